```python
import jax, jax.numpy as jnp
from jax import lax
import numpy as np

D_MODEL = 4096
BATCH = 4
SEQ = 4096
DEPTH = 1

GRID_W = 64
CTX_LEN = 256
GLA_VAL_W = D_MODEL // 2
GLA_HEADS = 8
GLA_DV = GLA_VAL_W // GLA_HEADS
GLA_DK = GLA_DV // 2
GLA_KEY_W = GLA_HEADS * GLA_DK
GLA_CHUNK = 64
GLA_LOWRANK = 16
GLA_TAU = 16.0
ROPE_BASE = 10000.0
SG_WIDTH = D_MODEL - GLA_VAL_W
SG_GROUPS = 4
SG_GROUP_W = SG_WIDTH // SG_GROUPS
SG_CHUNK = 128
MIX_W = GLA_VAL_W + SG_WIDTH
D_FF = 4 * D_MODEL
N_MOD = 6
EPS = 1e-6
Q0 = 0
K0 = Q0 + GLA_KEY_W
V0 = K0 + GLA_KEY_W
R0 = V0 + GLA_VAL_W
LF0 = R0 + GLA_VAL_W
LB0 = LF0 + GLA_LOWRANK
SG0 = LB0 + GLA_LOWRANK
IN_COLS = SG0 + 2 * SG_WIDTH

kernel_name = "hybrid_gla_gmlp_prefix_dit_block"


def rmsnorm(t, g):
    tf = t.astype(jnp.float32)
    y = tf * lax.rsqrt(jnp.mean(tf * tf, axis=-1, keepdims=True) + EPS)
    return (y * g.astype(jnp.float32)).astype(t.dtype)


def layernorm(t, g, b):
    tf = t.astype(jnp.float32)
    mu = jnp.mean(tf, axis=-1, keepdims=True)
    var = jnp.mean(jnp.square(tf - mu), axis=-1, keepdims=True)
    y = (tf - mu) * lax.rsqrt(var + EPS)
    return (y * g.astype(jnp.float32) + b.astype(jnp.float32)).astype(t.dtype)


def modulate(h, shift, scale):
    return h * (1.0 + scale) + shift


def split_heads(t, d):
    return t.reshape(t.shape[:-1] + (GLA_HEADS, d))


def flip_seq(t):
    return jnp.flip(t, axis=1)


def rope_axis(t, pos):
    m = t.shape[-1] // 2
    inv_freq = ROPE_BASE ** (-jnp.arange(m, dtype=jnp.float32) / m)
    ang = pos.astype(jnp.float32)[:, None] * inv_freq[None, :]
    cos = jnp.cos(ang)[:, None, :]
    sin = jnp.sin(ang)[:, None, :]
    t1 = t[..., :m].astype(jnp.float32)
    t2 = t[..., m:].astype(jnp.float32)
    return jnp.concatenate([t1 * cos - t2 * sin, t1 * sin + t2 * cos], axis=-1).astype(t.dtype)


def rope2d(t, row_pos, col_pos):
    half = t.shape[-1] // 2
    return jnp.concatenate([rope_axis(t[..., :half], row_pos), rope_axis(t[..., half:], col_pos)], axis=-1)


def gla_qk(z):
    q = split_heads(z[..., Q0:K0], GLA_DK) * (GLA_DK ** -0.5)
    k = split_heads(z[..., K0:V0], GLA_DK)
    return q, k


def gla_log_decay(lr, w_dec, b_dec):
    a = (lr @ w_dec + b_dec).astype(jnp.float32)
    return split_heads(jax.nn.log_sigmoid(a) / GLA_TAU, GLA_DK)


def gla_chunked(q, k, v, log_a, s0):
    bsz, n, h, dk = q.shape
    dv = v.shape[-1]
    nc = n // GLA_CHUNK

    def to_chunks(t):
        return t.astype(jnp.float32).reshape(bsz, nc, GLA_CHUNK, h, t.shape[-1]).transpose(1, 0, 3, 2, 4)

    mask = jnp.tril(jnp.ones((GLA_CHUNK, GLA_CHUNK), dtype=bool))[None, None, :, :, None]

    def step(s, inp):
        qi, ki, vi, ai = inp
        b = jnp.cumsum(ai, axis=2)
        inter = jnp.einsum('bhck,bhkv->bhcv', qi * jnp.exp(b), s)
        diff = b[:, :, :, None, :] - b[:, :, None, :, :]
        decay = jnp.exp(jnp.where(mask, diff, -jnp.inf))
        att = jnp.einsum('bhik,bhjk,bhijk->bhij', qi, ki, decay)
        intra = jnp.einsum('bhij,bhjv->bhiv', att, vi)
        b_last = b[:, :, -1, :]
        s_new = jnp.exp(b_last)[..., None] * s + jnp.einsum(
            'bhck,bhcv->bhkv', ki * jnp.exp(b_last[:, :, None, :] - b), vi)
        return s_new, inter + intra

    s_fin, o = lax.scan(step, s0.astype(jnp.float32), (to_chunks(q), to_chunks(k), to_chunks(v), to_chunks(log_a)))
    o = o.transpose(1, 0, 3, 2, 4).reshape(bsz, n, h, dv)
    return o, s_fin


def gla_final_state(k, v, log_a):
    b = jnp.cumsum(log_a.astype(jnp.float32), axis=1)
    w = jnp.exp(b[:, -1:] - b)
    return jnp.einsum('bnhk,bnhv->bhkv', k.astype(jnp.float32) * w, v.astype(jnp.float32))


def gla_bidir(q, k, v, la_f, la_b, s_f0, s_b0):
    o_f, s_f = gla_chunked(q, k, v, la_f, s_f0)
    o_b, s_b = gla_chunked(flip_seq(q), flip_seq(k), flip_seq(v), flip_seq(la_b), s_b0)
    return o_f + flip_seq(o_b), s_f, s_b


def gla_readout(o, r, g):
    bsz, n = o.shape[0], o.shape[1]
    y = rmsnorm(o, g).astype(r.dtype).reshape(bsz, n, GLA_VAL_W)
    return y * jax.nn.silu(r)


def spatial_gating(zs, ln_g, ln_b, w_s, b_s):
    zs = jax.nn.gelu(zs, approximate=False)
    u, vv = zs[..., :SG_WIDTH], zs[..., SG_WIDTH:]
    vv = layernorm(vv, ln_g, ln_b)
    bsz, n = vv.shape[0], vv.shape[1]
    vv = vv.reshape(bsz, n // SG_CHUNK, SG_CHUNK, SG_GROUPS, SG_GROUP_W)
    s = jnp.einsum('gij,bnjgc->bnigc', w_s, vv) + b_s.T[:, :, None]
    return u * s.reshape(bsz, n, SG_WIDTH)


def token_mix(z, q, k, s_f0, s_b0, w_dec_f, b_dec_f, w_dec_b, b_dec_b,
              gla_norm_g, sg_ln_g, sg_ln_b, w_s, b_s, w_o):
    v = split_heads(z[..., V0:R0], GLA_DV)
    la_f = gla_log_decay(z[..., LF0:LB0], w_dec_f, b_dec_f)
    la_b = gla_log_decay(z[..., LB0:SG0], w_dec_b, b_dec_b)
    o, s_f, s_b = gla_bidir(q, k, v, la_f, la_b, s_f0, s_b0)
    y_gla = gla_readout(o, z[..., R0:LF0], gla_norm_g)
    y_sg = spatial_gating(z[..., SG0:], sg_ln_g, sg_ln_b, w_s, b_s)
    y = jnp.concatenate([y_gla, y_sg], axis=-1) @ w_o
    return y, s_f, s_b


def ctx_states(hc, w_in, w_dec_f, b_dec_f, w_dec_b, b_dec_b):
    k = split_heads(hc @ w_in[:, K0:V0], GLA_DK)
    v = split_heads(hc @ w_in[:, V0:R0], GLA_DV)
    lr = hc @ w_in[:, LF0:SG0]
    la_f = gla_log_decay(lr[..., :GLA_LOWRANK], w_dec_f, b_dec_f)
    la_b = gla_log_decay(lr[..., GLA_LOWRANK:], w_dec_b, b_dec_b)
    s_f = gla_final_state(k, v, la_f)
    s_b = gla_final_state(flip_seq(k), flip_seq(v), flip_seq(la_b))
    return s_f, s_b


def sq_relu_mlp(h, w_1, w_2):
    return jnp.square(jax.nn.relu(h @ w_1)) @ w_2


def setup_inputs(seed: int = 0) -> dict:
    key = jax.random.key(seed)
    ks = jax.random.split(key, 24)

    def nrm(k, shape, scale):
        return jax.random.normal(k, shape, jnp.float32) * scale

    def gain(k, shape):
        return 1.0 + nrm(k, shape, 0.01)

    L = DEPTH
    return {
        "x": nrm(ks[0], (BATCH, SEQ, D_MODEL), 1.0),
        "c": nrm(ks[1], (BATCH, D_MODEL), 1.0),
        "ctx": nrm(ks[2], (BATCH, CTX_LEN, D_MODEL), 1.0),
        "c_ctx": nrm(ks[3], (D_MODEL,), 1.0),
        "w_ada": nrm(ks[4], (L, D_MODEL, N_MOD * D_MODEL), D_MODEL ** -0.5),
        "b_ada": nrm(ks[5], (L, N_MOD * D_MODEL), 0.01),
        "pre1_g": gain(ks[6], (L, D_MODEL)),
        "post1_g": gain(ks[7], (L, D_MODEL)),
        "pre2_g": gain(ks[8], (L, D_MODEL)),
        "post2_g": gain(ks[9], (L, D_MODEL)),
        "w_in": nrm(ks[10], (L, D_MODEL, IN_COLS), D_MODEL ** -0.5),
        "w_dec_f": nrm(ks[11], (L, GLA_LOWRANK, GLA_KEY_W), GLA_LOWRANK ** -0.5),
        "b_dec_f": nrm(ks[12], (L, GLA_KEY_W), 0.1),
        "w_dec_b": nrm(ks[13], (L, GLA_LOWRANK, GLA_KEY_W), GLA_LOWRANK ** -0.5),
        "b_dec_b": nrm(ks[14], (L, GLA_KEY_W), 0.1),
        "gla_norm_g": gain(ks[15], (L, GLA_HEADS, GLA_DV)),
        "sg_ln_g": gain(ks[16], (L, SG_WIDTH)),
        "sg_ln_b": nrm(ks[17], (L, SG_WIDTH), 0.01),
        "w_s": nrm(ks[18], (L, SG_GROUPS, SG_CHUNK, SG_CHUNK), SG_CHUNK ** -0.5),
        "b_s": gain(ks[19], (L, SG_GROUPS, SG_CHUNK)),
        "w_o": nrm(ks[20], (L, MIX_W, D_MODEL), MIX_W ** -0.5),
        "w_1": nrm(ks[21], (L, D_MODEL, D_FF), D_MODEL ** -0.5),
        "w_2": nrm(ks[22], (L, D_FF, D_MODEL), D_FF ** -0.5),
    }


def reference(x, c, ctx, c_ctx, w_ada, b_ada, pre1_g, post1_g, pre2_g, post2_g, w_in,
              w_dec_f, b_dec_f, w_dec_b, b_dec_b, gla_norm_g, sg_ln_g, sg_ln_b, w_s, b_s,
              w_o, w_1, w_2):
    bsz, n = x.shape[0], x.shape[1]
    ROWS = n // GRID_W
    pos = jnp.arange(ROWS * GRID_W)
    row_pos = pos // GRID_W
    col_pos = pos % GRID_W
    zero_state = jnp.zeros((bsz, GLA_HEADS, GLA_DK, GLA_DV), jnp.float32)
    cond_x = jax.nn.silu(c)[:, None, :]
    cond_c = jax.nn.silu(c_ctx)

    for l in range(DEPTH):
        mod_x = cond_x @ w_ada[l] + b_ada[l]
        sh1, sc1, g1, sh2, sc2, g2 = jnp.split(mod_x, N_MOD, axis=-1)
        hx = modulate(rmsnorm(x, pre1_g[l]), sh1, sc1)

        if l == DEPTH - 1:
            mod_c = cond_c @ w_ada[l][:, :2 * D_MODEL] + b_ada[l][:2 * D_MODEL]
            csh1, csc1 = jnp.split(mod_c, 2, axis=-1)
            hc = modulate(rmsnorm(ctx, pre1_g[l]), csh1, csc1)
            s_f, s_b = ctx_states(hc, w_in[l], w_dec_f[l], b_dec_f[l], w_dec_b[l], b_dec_b[l])
        else:
            mod_c = cond_c @ w_ada[l] + b_ada[l]
            csh1, csc1, cg1, csh2, csc2, cg2 = jnp.split(mod_c, N_MOD, axis=-1)
            hc = modulate(rmsnorm(ctx, pre1_g[l]), csh1, csc1)
            zc = hc @ w_in[l]
            qc, kc = gla_qk(zc)
            mix_c, s_f, s_b = token_mix(zc, qc, kc, zero_state, zero_state,
                                        w_dec_f[l], b_dec_f[l], w_dec_b[l], b_dec_b[l],
                                        gla_norm_g[l], sg_ln_g[l], sg_ln_b[l], w_s[l], b_s[l], w_o[l])
            ctx = ctx + cg1 * rmsnorm(mix_c, post1_g[l])
            hc2 = modulate(rmsnorm(ctx, pre2_g[l]), csh2, csc2)
            ctx = ctx + cg2 * rmsnorm(sq_relu_mlp(hc2, w_1[l], w_2[l]), post2_g[l])

        zx = hx @ w_in[l]
        qx, kx = gla_qk(zx)
        qx = rope2d(qx, row_pos, col_pos)
        kx = rope2d(kx, row_pos, col_pos)
        mix_x, _, _ = token_mix(zx, qx, kx, s_f, s_b,
                                w_dec_f[l], b_dec_f[l], w_dec_b[l], b_dec_b[l],
                                gla_norm_g[l], sg_ln_g[l], sg_ln_b[l], w_s[l], b_s[l], w_o[l])
        x = x + g1 * rmsnorm(mix_x, post1_g[l])
        h2 = modulate(rmsnorm(x, pre2_g[l]), sh2, sc2)
        x = x + g2 * rmsnorm(sq_relu_mlp(h2, w_1[l], w_2[l]), post2_g[l])
    return x
```

```python
import functools
import math

import jax
import jax.numpy as jnp
from jax import lax
from jax.experimental import pallas as pl
from jax.experimental.pallas import tpu as pltpu

GRID_W = 64
GLA_HEADS = 8
GLA_TAU = 16.0
ROPE_BASE = 10000.0
N_MOD = 6
EPS = 1e-6
SG_CHUNK = 128
GLA_CHUNK = 64

V7X_LANES = 128
MOD_ROWS = 8
VMEM_LIMIT_BYTES = 56 * 1024 * 1024

F32 = jnp.float32
BF16 = jnp.bfloat16


def _params(sem, vmem=VMEM_LIMIT_BYTES):
    return pltpu.CompilerParams(dimension_semantics=sem, vmem_limit_bytes=vmem)


def _rms(t, g):
    return t * lax.rsqrt(jnp.mean(t * t, axis=-1, keepdims=True) + EPS) * g


def _silu(t):
    return t * (1.0 / (1.0 + jnp.exp(-t)))


def _ada_kernel(c_ref, w_ref, b_ref, o_ref):
    cond = _silu(c_ref[...])
    o_ref[...] = jnp.dot(cond.astype(BF16), w_ref[...].astype(BF16),
                         preferred_element_type=F32) + b_ref[...]


def _ada(c_all, w_ada, b_ada):
    d, n = w_ada.shape
    bn = min(n, 1024)
    return pl.pallas_call(
        _ada_kernel,
        grid=(n // bn,),
        in_specs=[pl.BlockSpec((MOD_ROWS, d), lambda j: (0, 0)),
                  pl.BlockSpec((d, bn), lambda j: (0, j)),
                  pl.BlockSpec((1, bn), lambda j: (0, j))],
        out_specs=pl.BlockSpec((MOD_ROWS, bn), lambda j: (0, j)),
        out_shape=jax.ShapeDtypeStruct((MOD_ROWS, n), F32),
        compiler_params=_params(("arbitrary",)),
        name="ada_mod",
    )(c_all, w_ada, b_ada.reshape(1, n))


def _prenorm_kernel(x_ref, ctx_ref, mod_ref, g_ref, o_ref, *, n_ctx_blk, d):
    j = pl.program_id(1)

    def emit(t):
        y = _rms(t, g_ref[...])
        o_ref[...] = (y * (1.0 + mod_ref[:, d:2 * d]) + mod_ref[:, 0:d]).astype(o_ref.dtype)

    @pl.when(j < n_ctx_blk)
    def _():
        emit(ctx_ref[...])

    @pl.when(j >= n_ctx_blk)
    def _():
        emit(x_ref[...])


def _prenorm(x, ctx, mod3, g, rb):
    bsz, s, d = x.shape
    c = ctx.shape[1]
    n_ctx_blk, nb = c // rb, (c + s) // rb
    return pl.pallas_call(
        functools.partial(_prenorm_kernel, n_ctx_blk=n_ctx_blk, d=d),
        grid=(bsz, nb),
        in_specs=[
            pl.BlockSpec((None, rb, d), lambda b, j: (b, jnp.maximum(j - n_ctx_blk, 0), 0)),
            pl.BlockSpec((None, rb, d), lambda b, j: (b, jnp.minimum(j, n_ctx_blk - 1), 0)),
            pl.BlockSpec((None, 1, 2 * d), lambda b, j: (jnp.where(j < n_ctx_blk, bsz, b), 0, 0)),
            pl.BlockSpec((1, d), lambda b, j: (0, 0)),
        ],
        out_specs=pl.BlockSpec((rb, d), lambda b, j: (b * nb + j, 0)),
        out_shape=jax.ShapeDtypeStruct((bsz * (c + s), d), BF16),
        compiler_params=_params(("arbitrary", "arbitrary")),
        name="prenorm_mod",
    )(x, ctx, mod3, g)


def _inproj_kernel(a_ref, w_ref, wlr_ref, z_ref, lr_ref):
    a = a_ref[...]
    z_ref[...] = jnp.dot(a, w_ref[...], preferred_element_type=F32).astype(z_ref.dtype)

    @pl.when(pl.program_id(1) == 0)
    def _():
        lr_ref[...] = jnp.dot(a, wlr_ref[...], preferred_element_type=F32)


def _inproj(a, w, wlr, bm, bn):
    m, k = a.shape
    n = w.shape[1]
    return pl.pallas_call(
        _inproj_kernel,
        grid=(m // bm, n // bn),
        in_specs=[pl.BlockSpec((bm, k), lambda i, j: (i, 0)),
                  pl.BlockSpec((k, bn), lambda i, j: (0, j)),
                  pl.BlockSpec((k, V7X_LANES), lambda i, j: (0, 0))],
        out_specs=[pl.BlockSpec((bm, bn), lambda i, j: (i, j)),
                   pl.BlockSpec((bm, V7X_LANES), lambda i, j: (i, 0))],
        out_shape=[jax.ShapeDtypeStruct((m, n), BF16),
                   jax.ShapeDtypeStruct((m, V7X_LANES), F32)],
        compiler_params=_params(("arbitrary", "arbitrary")),
        name="in_proj",
    )(a, w, wlr)


def _matmul_kernel(a_ref, w_ref, o_ref, *acc, nk, relu2):
    def finish(r):
        if relu2:
            r = jnp.square(jnp.maximum(r, 0.0))
        o_ref[...] = r.astype(o_ref.dtype)

    if nk == 1:
        finish(jnp.dot(a_ref[...], w_ref[...], preferred_element_type=F32))
        return
    acc_ref, = acc
    kk = pl.program_id(2)

    @pl.when(kk == 0)
    def _():
        acc_ref[...] = jnp.zeros_like(acc_ref)

    acc_ref[...] += jnp.dot(a_ref[...], w_ref[...], preferred_element_type=F32)

    @pl.when(kk == nk - 1)
    def _():
        finish(acc_ref[...])


def _matmul(a, w, out_dtype, bm, bn, bk, relu2=False, name="matmul"):
    m, k = a.shape
    n = w.shape[1]
    nk = k // bk
    return pl.pallas_call(
        functools.partial(_matmul_kernel, nk=nk, relu2=relu2),
        grid=(m // bm, n // bn, nk),
        in_specs=[pl.BlockSpec((bm, bk), lambda i, j, kk: (i, kk)),
                  pl.BlockSpec((bk, bn), lambda i, j, kk: (kk, j))],
        out_specs=pl.BlockSpec((bm, bn), lambda i, j, kk: (i, j)),
        out_shape=jax.ShapeDtypeStruct((m, n), out_dtype),
        scratch_shapes=[pltpu.VMEM((bm, bn), F32)] if nk > 1 else [],
        compiler_params=_params(("arbitrary", "arbitrary", "arbitrary")),
        name=name,
    )(a, w)


def _gla_dir(q_ref, k_ref, v_ref, lr_ref, cos_ref, sin_ref, wd_ref, bd_ref, mask_ref,
             o_ref, st_ref, emit_out, *, hb, dk, dv, tb, reverse):
    nch = tb // GLA_CHUNK
    cos = cos_ref[...]
    sin = sin_ref[...]
    mask = mask_ref[...]
    tri = mask.astype(BF16)
    lane = lax.broadcasted_iota(jnp.int32, (tb, dk), 1)
    first = (lane & (dk // 4)) == 0
    lr = lr_ref[...].astype(BF16)

    def rope(t):
        sw = jnp.where(first, pltpu.roll(t, dk - dk // 4, 1), pltpu.roll(t, dk // 4, 1))
        return t * cos + sw * sin

    chunks = range(nch - 1, -1, -1) if reverse else range(nch)
    for h in range(hb):
        ks = slice(h * dk, (h + 1) * dk)
        vs = slice(h * dv, (h + 1) * dv)
        a = jnp.dot(lr, wd_ref[:, ks].astype(BF16), preferred_element_type=F32) + bd_ref[:, ks]
        la = (jnp.minimum(a, 0.0) - jnp.log1p(jnp.exp(-jnp.abs(a)))) * (1.0 / GLA_TAU)
        la_hi = la.astype(BF16)
        la_lo = (la - la_hi.astype(F32)).astype(BF16)
        cum = (jnp.dot(tri, la_hi, preferred_element_type=F32)
               + jnp.dot(tri, la_lo, preferred_element_type=F32))
        kr = rope(k_ref[:, ks].astype(F32))
        v = v_ref[:, vs]
        st = st_ref[h]
        if emit_out:
            qe = (rope(q_ref[:, ks].astype(F32)) * (jnp.exp(cum) * (dk ** -0.5))).astype(BF16)
            ke = (kr * jnp.exp(-cum)).astype(BF16)
            att = lax.dot_general(qe, ke, (((1,), (1,)), ((), ())), preferred_element_type=F32)
            att = jnp.where(mask > 0.5, att, 0.0).astype(BF16)
            o_intra = jnp.dot(att, v, preferred_element_type=F32)
        for c in chunks:
            rows = slice(c * GLA_CHUNK, (c + 1) * GLA_CHUNK)
            last = c * GLA_CHUNK if reverse else (c + 1) * GLA_CHUNK - 1
            cl = cum[last:last + 1, :]
            if emit_out:
                inter = lax.dot_general(qe[rows], st.astype(BF16), (((1,), (1,)), ((), ())),
                                        preferred_element_type=F32)
                o_ref[rows, vs] = (o_intra[rows] + inter).astype(o_ref.dtype)
            kd = (kr[rows] * jnp.exp(cl - cum[rows])).astype(BF16)
            upd = lax.dot_general(v[rows], kd, (((0,), (0,)), ((), ())), preferred_element_type=F32)
            st = st * jnp.exp(cl) + upd
        st_ref[h] = st


def _gla_kernel(qf, kf, vf, lrf, cosf, sinf, qb, kb, vb, lrb, cosb, sinb,
                wdf, bdf, wdb, bdb, maskf, maskb, of_ref, ob_ref, stf, stb,
                *, n_ctx_blk, **kw):
    s = pl.program_id(2)

    @pl.when(s == 0)
    def _():
        stf[...] = jnp.zeros_like(stf)
        stb[...] = jnp.zeros_like(stb)

    @pl.when(s < n_ctx_blk)
    def _():
        _gla_dir(qf, kf, vf, lrf, cosf, sinf, wdf, bdf, maskf, of_ref, stf, False, reverse=False, **kw)
        _gla_dir(qb, kb, vb, lrb, cosb, sinb, wdb, bdb, maskb, ob_ref, stb, False, reverse=True, **kw)

    @pl.when(s >= n_ctx_blk)
    def _():
        _gla_dir(qf, kf, vf, lrf, cosf, sinf, wdf, bdf, maskf, of_ref, stf, True, reverse=False, **kw)
        _gla_dir(qb, kb, vb, lrb, cosb, sinb, wdb, bdb, maskb, ob_ref, stb, True, reverse=True, **kw)


def _gla(z, lr, cos_t, sin_t, wdf, bdf, wdb, bdb, maskf, maskb, *, bsz, s_len, c_len, tb, hb, dk, dv):
    n_ctx_blk, n_x_blk = c_len // tb, s_len // tb
    nb = n_ctx_blk + n_x_blk
    key_w = GLA_HEADS * dk
    qw, vw = hb * dk, hb * dv
    k_off, v_off = key_w // qw, (2 * key_w) // vw

    def fblk(s):
        return s

    def bblk(s):
        return jnp.where(s < n_ctx_blk, n_ctx_blk - 1 - s, nb - 1 - s + n_ctx_blk)

    def specs(blk):
        return [
            pl.BlockSpec((tb, qw), lambda b, g, s: (b * nb + blk(s), g)),
            pl.BlockSpec((tb, qw), lambda b, g, s: (b * nb + blk(s), k_off + g)),
            pl.BlockSpec((tb, vw), lambda b, g, s: (b * nb + blk(s), v_off + g)),
            pl.BlockSpec((tb, V7X_LANES), lambda b, g, s: (b * nb + blk(s), 0)),
            pl.BlockSpec((tb, dk), lambda b, g, s: (blk(s), 0)),
            pl.BlockSpec((tb, dk), lambda b, g, s: (blk(s), 0)),
        ]

    def wspec():
        return [pl.BlockSpec((V7X_LANES, qw), lambda b, g, s: (0, g)),
                pl.BlockSpec((1, qw), lambda b, g, s: (0, g))]

    def oblk_f(b, g, s):
        return (b * n_x_blk + jnp.maximum(s - n_ctx_blk, 0), g)

    def oblk_b(b, g, s):
        return (b * n_x_blk + jnp.minimum(nb - 1 - s + n_ctx_blk, nb - 1) - n_ctx_blk, g)

    mspec = pl.BlockSpec((tb, tb), lambda b, g, s: (0, 0))
    kern = functools.partial(_gla_kernel, n_ctx_blk=n_ctx_blk, hb=hb, dk=dk, dv=dv, tb=tb)
    o_shape = jax.ShapeDtypeStruct((bsz * s_len, GLA_HEADS * dv), BF16)
    return pl.pallas_call(
        kern,
        grid=(bsz, GLA_HEADS // hb, nb),
        in_specs=specs(fblk) + specs(bblk) + wspec() + wspec() + [mspec, mspec],
        out_specs=[pl.BlockSpec((tb, vw), oblk_f), pl.BlockSpec((tb, vw), oblk_b)],
        out_shape=[o_shape, o_shape],
        scratch_shapes=[pltpu.VMEM((hb, dv, dk), F32), pltpu.VMEM((hb, dv, dk), F32)],
        compiler_params=_params(("arbitrary", "arbitrary", "arbitrary")),
        name="gla_scan",
    )(z, z, z, lr, cos_t, sin_t, z, z, z, lr, cos_t, sin_t, wdf, bdf, wdb, bdb, maskf, maskb)


def _mix_kernel(of_ref, ob_ref, r_ref, u_ref, vv_ref, gn_ref, lg_ref, lb_ref, ws_ref, bs_ref, y_ref,
                *, dv, val_w, groups, tb):
    o = of_ref[...].astype(F32) + ob_ref[...].astype(F32)
    for h in range(val_w // dv):
        cs = slice(h * dv, (h + 1) * dv)
        r = r_ref[:, cs].astype(F32)
        y_ref[:, cs] = (_rms(o[:, cs], gn_ref[:, cs]) * _silu(r)).astype(y_ref.dtype)

    def gelu(t):
        return 0.5 * t * (1.0 + lax.erf(t * (2.0 ** -0.5)))

    u = gelu(u_ref[...].astype(F32))
    vv = gelu(vv_ref[...].astype(F32))
    mu = jnp.mean(vv, axis=-1, keepdims=True)
    cen = vv - mu
    var = jnp.mean(cen * cen, axis=-1, keepdims=True)
    vn = (cen * lax.rsqrt(var + EPS) * lg_ref[...] + lb_ref[...]).astype(BF16)
    sgw = vn.shape[1]
    gw = sgw // groups
    for c in range(tb // SG_CHUNK):
        rows = slice(c * SG_CHUNK, (c + 1) * SG_CHUNK)
        for g in range(groups):
            cs = slice(g * gw, (g + 1) * gw)
            sg = jnp.dot(ws_ref[g], vn[rows, cs], preferred_element_type=F32) + bs_ref[:, cs]
            y_ref[rows, val_w + g * gw:val_w + (g + 1) * gw] = (u[rows, cs] * sg).astype(y_ref.dtype)


def _mix(o_f, o_b, z, gn, lg, lb, ws, bsx, *, bsz, s_len, c_len, tb, dv, val_w, sgw, r_off):
    t = bsz * s_len
    n_x, nb = s_len // tb, (s_len + c_len) // tb
    groups = ws.shape[0]

    def zrow(i):
        return (i // n_x) * nb + c_len // tb + i % n_x

    kern = functools.partial(_mix_kernel, dv=dv, val_w=val_w, groups=groups, tb=tb)
    return pl.pallas_call(
        kern,
        grid=(t // tb,),
        in_specs=[pl.BlockSpec((tb, val_w), lambda i: (i, 0)),
                  pl.BlockSpec((tb, val_w), lambda i: (i, 0)),
                  pl.BlockSpec((tb, val_w), lambda i: (zrow(i), r_off // val_w)),
                  pl.BlockSpec((tb, sgw), lambda i: (zrow(i), (r_off + val_w) // sgw)),
                  pl.BlockSpec((tb, sgw), lambda i: (zrow(i), (r_off + val_w) // sgw + 1)),
                  pl.BlockSpec((1, val_w), lambda i: (0, 0)),
                  pl.BlockSpec((1, sgw), lambda i: (0, 0)),
                  pl.BlockSpec((1, sgw), lambda i: (0, 0)),
                  pl.BlockSpec((groups, SG_CHUNK, SG_CHUNK), lambda i: (0, 0, 0)),
                  pl.BlockSpec((SG_CHUNK, sgw), lambda i: (0, 0))],
        out_specs=pl.BlockSpec((tb, val_w + sgw), lambda i: (i, 0)),
        out_shape=jax.ShapeDtypeStruct((t, val_w + sgw), BF16),
        compiler_params=_params(("arbitrary",)),
        name="mix_readout",
    )(o_f, o_b, z, z, z, gn, lg, lb, ws, bsx)


def _res1_kernel(x_ref, m_ref, g1_ref, sh_ref, sc_ref, pg_ref, ng_ref, x1_ref, h2_ref):
    x1 = x_ref[...] + g1_ref[...] * _rms(m_ref[...].astype(F32), pg_ref[...])
    x1_ref[...] = x1
    h2_ref[...] = (_rms(x1, ng_ref[...]) * (1.0 + sc_ref[...]) + sh_ref[...]).astype(h2_ref.dtype)


def _res1(x2, mix, mod3, post_g, pre_g, *, s_len, tb):
    t, d = x2.shape
    n_x = s_len // tb
    row = pl.BlockSpec((tb, d), lambda i: (i, 0))
    vec = pl.BlockSpec((1, d), lambda i: (0, 0))

    def modc(col):
        return pl.BlockSpec((None, 1, d), lambda i: (i // n_x, 0, col))

    return pl.pallas_call(
        _res1_kernel,
        grid=(t // tb,),
        in_specs=[row, row, modc(2), modc(3), modc(4), vec, vec],
        out_specs=[row, row],
        out_shape=[jax.ShapeDtypeStruct((t, d), F32), jax.ShapeDtypeStruct((t, d), BF16)],
        compiler_params=_params(("arbitrary",)),
        name="residual1",
    )(x2, mix, mod3, mod3, mod3, post_g, pre_g)


def _res2_kernel(x_ref, m_ref, g2_ref, pg_ref, o_ref):
    o_ref[...] = x_ref[...] + g2_ref[...] * _rms(m_ref[...].astype(F32), pg_ref[...])


def _res2(x1, m2, mod3, post_g, *, s_len, tb):
    t, d = x1.shape
    n_x = s_len // tb
    row = pl.BlockSpec((tb, d), lambda i: (i, 0))
    return pl.pallas_call(
        _res2_kernel,
        grid=(t // tb,),
        in_specs=[row, row, pl.BlockSpec((None, 1, d), lambda i: (i // n_x, 0, 5)),
                  pl.BlockSpec((1, d), lambda i: (0, 0))],
        out_specs=row,
        out_shape=jax.ShapeDtypeStruct((t, d), F32),
        compiler_params=_params(("arbitrary",)),
        name="residual2",
    )(x1, m2, mod3, post_g)


def _rope_tables(s_len, c_len, dk):
    m = dk // 4
    inv_freq = ROPE_BASE ** (-jnp.arange(m, dtype=F32) / m)
    pos = jnp.arange(s_len)
    ang_r = (pos // GRID_W).astype(F32)[:, None] * inv_freq[None, :]
    ang_c = (pos % GRID_W).astype(F32)[:, None] * inv_freq[None, :]
    cos = jnp.concatenate([jnp.cos(ang_r)] * 2 + [jnp.cos(ang_c)] * 2, axis=-1)
    sin = jnp.concatenate([-jnp.sin(ang_r), jnp.sin(ang_r), -jnp.sin(ang_c), jnp.sin(ang_c)], axis=-1)
    cos = jnp.concatenate([jnp.ones((c_len, dk), F32), cos], axis=0)
    sin = jnp.concatenate([jnp.zeros((c_len, dk), F32), sin], axis=0)
    return cos, sin


def _chunk_masks(tb):
    i = jnp.arange(tb)
    same = (i[:, None] // GLA_CHUNK) == (i[None, :] // GLA_CHUNK)
    return ((same & (i[None, :] <= i[:, None])).astype(F32),
            (same & (i[None, :] >= i[:, None])).astype(F32))


def _tile(n, pref):
    return pref if n % pref == 0 else n


def kernel(x, c, ctx, c_ctx, w_ada, b_ada, pre1_g, post1_g, pre2_g, post2_g, w_in, w_dec_f, b_dec_f,
           w_dec_b, b_dec_b, gla_norm_g, sg_ln_g, sg_ln_b, w_s, b_s, w_o, w_1, w_2):
    bsz, s_len, d = x.shape
    c_len = ctx.shape[1]
    assert w_ada.shape[0] == 1, "single-layer block only"
    assert bsz < MOD_ROWS
    lowrank, key_w = w_dec_f.shape[1], w_dec_f.shape[2]
    dk = key_w // GLA_HEADS
    dv = gla_norm_g.shape[2]
    val_w = GLA_HEADS * dv
    sgw = sg_ln_g.shape[1]
    lf0 = 2 * key_w + 2 * val_w
    sg0 = lf0 + 2 * lowrank
    assert dk == V7X_LANES and 2 * lowrank <= V7X_LANES and w_in.shape[2] == sg0 + 2 * sgw
    t = bsz * s_len

    w_in0 = w_in[0]
    w_main = jnp.concatenate([w_in0[:, :lf0], w_in0[:, sg0:]], axis=1).astype(BF16)
    w_lr = jnp.pad(w_in0[:, lf0:sg0], ((0, 0), (0, V7X_LANES - 2 * lowrank))).astype(BF16)
    w_o_b, w_1_b, w_2_b = w_o[0].astype(BF16), w_1[0].astype(BF16), w_2[0].astype(BF16)
    wdf = jnp.pad(w_dec_f[0], ((0, V7X_LANES - lowrank), (0, 0)))
    wdb = jnp.pad(w_dec_b[0], ((lowrank, V7X_LANES - 2 * lowrank), (0, 0)))

    c_all = jnp.concatenate([c, c_ctx[None], jnp.zeros((MOD_ROWS - bsz - 1, d), F32)], axis=0)
    mod = _ada(c_all, w_ada[0], b_ada[0])
    mod3 = mod.reshape(MOD_ROWS, 1, N_MOD * d)

    rb = _tile(math.gcd(s_len, c_len), 256)
    hx = _prenorm(x, ctx, mod3, pre1_g, rb)

    m_all = hx.shape[0]
    z, lr = _inproj(hx, w_main, w_lr, _tile(m_all, 1024), _tile(w_main.shape[1], 1024))

    cos_t, sin_t = _rope_tables(s_len, c_len, dk)
    maskf, maskb = _chunk_masks(rb)
    hb = 2
    o_f, o_b = _gla(z, lr, cos_t, sin_t, wdf, b_dec_f, wdb, b_dec_b, maskf, maskb,
                    bsz=bsz, s_len=s_len, c_len=c_len, tb=rb, hb=hb, dk=dk, dv=dv)

    bsx = jnp.repeat(b_s[0].T, sgw // w_s.shape[1], axis=1)
    y = _mix(o_f, o_b, z, gla_norm_g.reshape(1, val_w), sg_ln_g, sg_ln_b, w_s[0].astype(BF16), bsx,
             bsz=bsz, s_len=s_len, c_len=c_len, tb=rb, dv=dv, val_w=val_w, sgw=sgw, r_off=lf0 - val_w)

    bm = _tile(t, 1024)
    mix = _matmul(y, w_o_b, F32, bm, _tile(d, 1024), d, name="out_proj")
    x2 = x.reshape(t, d)
    x1, h2 = _res1(x2, mix, mod3, post1_g, pre2_g, s_len=s_len, tb=rb)
    d_ff = w_1_b.shape[1]
    hmid = _matmul(h2, w_1_b, BF16, bm, _tile(d_ff, 1024), d, relu2=True, name="mlp_up")
    m2 = _matmul(hmid, w_2_b, F32, bm, _tile(d, 1024), _tile(d_ff, 2048), name="mlp_down")
    out = _res2(x1, m2, mod3, post2_g, s_len=s_len, tb=rb)
    return out.reshape(bsz, s_len, d)
```

```python
import collections
import functools
import math

import jax
import jax.numpy as jnp
from jax import lax
from jax.experimental import pallas as pl
from jax.experimental.pallas import tpu as pltpu

GRID_W = 64
GLA_HEADS = 8
GLA_TAU = 16.0
ROPE_BASE = 10000.0
N_MOD = 6
EPS = 1e-6
SG_CHUNK = 128
GLA_CHUNK = 64

V7X_LANES = 128
MOD_ROWS = 8
VMEM_LIMIT_BYTES = 56 * 1024 * 1024

F32 = jnp.float32
BF16 = jnp.bfloat16


def _params(sem, vmem=VMEM_LIMIT_BYTES):
    return pltpu.CompilerParams(dimension_semantics=sem, vmem_limit_bytes=vmem)


def _rms(t, g):
    return t * lax.rsqrt(jnp.mean(t * t, axis=-1, keepdims=True) + EPS) * g


def _silu(t):
    return t * (1.0 / (1.0 + jnp.exp(-t)))


def _gelu(t):
    return 0.5 * t * (1.0 + lax.erf(t * (2.0 ** -0.5)))


def _ada_kernel(c_ref, w_ref, b_ref, o_ref):
    cond = _silu(c_ref[...])
    o_ref[...] = jnp.dot(cond.astype(BF16), w_ref[...].astype(BF16),
                         preferred_element_type=F32) + b_ref[...]


def _ada(c_all, w_ada, b_ada):
    d, n = w_ada.shape
    bn = min(n, 1024)
    return pl.pallas_call(
        _ada_kernel,
        grid=(n // bn,),
        in_specs=[pl.BlockSpec((MOD_ROWS, d), lambda j: (0, 0)),
                  pl.BlockSpec((d, bn), lambda j: (0, j)),
                  pl.BlockSpec((1, bn), lambda j: (0, j))],
        out_specs=pl.BlockSpec((MOD_ROWS, bn), lambda j: (0, j)),
        out_shape=jax.ShapeDtypeStruct((MOD_ROWS, n), F32),
        compiler_params=_params(("arbitrary",)),
        name="ada_mod",
    )(c_all, w_ada, b_ada.reshape(1, n))


def _prenorm_kernel(x_ref, ctx_ref, mod_ref, g_ref, o_ref, *, n_ctx_blk, d):
    j = pl.program_id(1)

    def emit(t):
        y = _rms(t, g_ref[...])
        o_ref[...] = (y * (1.0 + mod_ref[:, d:2 * d]) + mod_ref[:, 0:d]).astype(o_ref.dtype)

    @pl.when(j < n_ctx_blk)
    def _():
        emit(ctx_ref[...])

    @pl.when(j >= n_ctx_blk)
    def _():
        emit(x_ref[...])


def _prenorm(x, ctx, mod3, g, rb):
    bsz, s, d = x.shape
    c = ctx.shape[1]
    n_ctx_blk, nb = c // rb, (c + s) // rb
    return pl.pallas_call(
        functools.partial(_prenorm_kernel, n_ctx_blk=n_ctx_blk, d=d),
        grid=(bsz, nb),
        in_specs=[
            pl.BlockSpec((None, rb, d), lambda b, j: (b, jnp.maximum(j - n_ctx_blk, 0), 0)),
            pl.BlockSpec((None, rb, d), lambda b, j: (b, jnp.minimum(j, n_ctx_blk - 1), 0)),
            pl.BlockSpec((None, 1, 2 * d), lambda b, j: (jnp.where(j < n_ctx_blk, bsz, b), 0, 0)),
            pl.BlockSpec((1, d), lambda b, j: (0, 0)),
        ],
        out_specs=pl.BlockSpec((rb, d), lambda b, j: (b * nb + j, 0)),
        out_shape=jax.ShapeDtypeStruct((bsz * (c + s), d), BF16),
        compiler_params=_params(("arbitrary", "arbitrary")),
        name="prenorm_mod",
    )(x, ctx, mod3, g)


def _inproj_kernel(a_ref, w_ref, wlr_ref, z_ref, lr_ref):
    a = a_ref[...]
    z_ref[...] = jnp.dot(a, w_ref[...], preferred_element_type=F32).astype(z_ref.dtype)

    @pl.when(pl.program_id(1) == 0)
    def _():
        lr_ref[...] = jnp.dot(a, wlr_ref[...], preferred_element_type=F32)


def _inproj(a, w, wlr, bm, bn):
    m, k = a.shape
    n = w.shape[1]
    return pl.pallas_call(
        _inproj_kernel,
        grid=(m // bm, n // bn),
        in_specs=[pl.BlockSpec((bm, k), lambda i, j: (i, 0)),
                  pl.BlockSpec((k, bn), lambda i, j: (0, j)),
                  pl.BlockSpec((k, V7X_LANES), lambda i, j: (0, 0))],
        out_specs=[pl.BlockSpec((bm, bn), lambda i, j: (i, j)),
                   pl.BlockSpec((bm, V7X_LANES), lambda i, j: (i, 0))],
        out_shape=[jax.ShapeDtypeStruct((m, n), BF16),
                   jax.ShapeDtypeStruct((m, V7X_LANES), F32)],
        compiler_params=_params(("arbitrary", "arbitrary")),
        name="in_proj",
    )(a, w, wlr)


def _matmul_kernel(a_ref, w_ref, o_ref, *acc, nk, act):
    def finish(r):
        if act == "relu2":
            r = jnp.square(jnp.maximum(r, 0.0))
        elif act == "gelu":
            r = _gelu(r)
        o_ref[...] = r.astype(o_ref.dtype)

    if nk == 1:
        finish(jnp.dot(a_ref[...], w_ref[...], preferred_element_type=F32))
        return
    acc_ref, = acc
    kk = pl.program_id(2)

    @pl.when(kk == 0)
    def _():
        acc_ref[...] = jnp.zeros_like(acc_ref)

    acc_ref[...] += jnp.dot(a_ref[...], w_ref[...], preferred_element_type=F32)

    @pl.when(kk == nk - 1)
    def _():
        finish(acc_ref[...])


def _matmul(a, w, out_dtype, bm, bn, bk, act=None, name="matmul"):
    m, k = a.shape
    n = w.shape[1]
    nk = k // bk
    return pl.pallas_call(
        functools.partial(_matmul_kernel, nk=nk, act=act),
        grid=(m // bm, n // bn, nk),
        in_specs=[pl.BlockSpec((bm, bk), lambda i, j, kk: (i, kk)),
                  pl.BlockSpec((bk, bn), lambda i, j, kk: (kk, j))],
        out_specs=pl.BlockSpec((bm, bn), lambda i, j, kk: (i, j)),
        out_shape=jax.ShapeDtypeStruct((m, n), out_dtype),
        scratch_shapes=[pltpu.VMEM((bm, bn), F32)] if nk > 1 else [],
        compiler_params=_params(("arbitrary", "arbitrary", "arbitrary")),
        name=name,
    )(a, w)


_Dir = collections.namedtuple("_Dir", "q k v lr cos sin wd bd cmat o st reverse")


def _gla_block(dirs, emit, *, hb, dk, dv, tb):
    nch = tb // GLA_CHUNK
    w = hb * dk
    lane = lax.broadcasted_iota(jnp.int32, (tb, w), 1)
    first = (lane & (dk // 4)) == 0
    nt = (((1,), (1,)), ((), ()))
    tn = (((0,), (0,)), ((), ()))

    def ks(h):
        return slice(h * dk, (h + 1) * dk)

    def vs(h):
        return slice(h * dv, (h + 1) * dv)

    def rows(c):
        return slice(c * GLA_CHUNK, (c + 1) * GLA_CHUNK)

    def rope(t, cos, sin):
        sw = jnp.where(first, pltpu.roll(t, w - dk // 4, 1), pltpu.roll(t, dk // 4, 1))
        return t * cos + sw * sin

    gates = []
    for d in dirs:
        a = jnp.dot(d.lr[...].astype(BF16), d.wd[...].astype(BF16), preferred_element_type=F32) + d.bd[...]
        la = (jnp.minimum(a, 0.0) - jnp.log1p(jnp.exp(-jnp.abs(a)))) * (1.0 / GLA_TAU)
        hi = la.astype(BF16)
        lo = (la - hi.astype(F32)).astype(BF16)
        cm = d.cmat[...]
        cs = jnp.dot(cm, hi, preferred_element_type=F32) + jnp.dot(cm, lo, preferred_element_type=F32)
        gates.append((cs[:tb], cs[tb:], cm[:tb]))

    work = []
    for d, (cum, tot, tri) in zip(dirs, gates):
        cos, sin = d.cos[...], d.sin[...]
        kr = rope(d.k[...].astype(F32), cos, sin)
        kd = (kr * jnp.exp(tot - cum)).astype(BF16)
        qe = ke = None
        if emit:
            qe = (rope(d.q[...].astype(F32), cos, sin) * (jnp.exp(cum) * dk ** -0.5)).astype(BF16)
            ke = (kr * jnp.exp(-cum)).astype(BF16)
        work.append((d, d.v[...], qe, ke, kd, tot, tri))

    heads = range(hb)
    intra = {}
    if emit:
        att = {(i, h): lax.dot_general(qe[:, ks(h)], ke[:, ks(h)], nt, preferred_element_type=F32)
               for i, (_, _, qe, ke, _, _, _) in enumerate(work) for h in heads}
        for i, (_, v, _, _, _, _, tri) in enumerate(work):
            for h in heads:
                att_m = jnp.where(tri > 0, att[i, h].astype(BF16), jnp.zeros((), BF16))
                intra[i, h] = jnp.dot(att_m, v[:, vs(h)], preferred_element_type=F32)

    upd = {(i, h, c): lax.dot_general(v[rows(c), vs(h)], kd[rows(c), ks(h)], tn, preferred_element_type=F32)
           for i, (_, v, _, _, kd, _, _) in enumerate(work) for h in heads for c in range(nch)}

    seen = {}
    for i, (d, _, _, _, _, tot, _) in enumerate(work):
        order = range(nch - 1, -1, -1) if d.reverse else range(nch)
        for h in heads:
            st = d.st[h]
            for c in order:
                if emit:
                    seen[i, h, c] = st.astype(BF16)
                st = st * jnp.exp(tot[c * GLA_CHUNK:c * GLA_CHUNK + 1, ks(h)]) + upd[i, h, c]
            d.st[h] = st

    if emit:
        for i, (d, _, qe, _, _, _, _) in enumerate(work):
            for h in heads:
                for c in range(nch):
                    inter = lax.dot_general(qe[rows(c), ks(h)], seen[i, h, c], nt, preferred_element_type=F32)
                    d.o[rows(c), vs(h)] = (intra[i, h][rows(c)] + inter).astype(d.o.dtype)


def _gla_kernel(qf, kf, vf, lrf, cosf, sinf, qb, kb, vb, lrb, cosb, sinb,
                wdf, bdf, wdb, bdb, cmf, cmb, of_ref, ob_ref, stf, stb, *, n_ctx_blk, **kw):
    s = pl.program_id(2)
    dirs = (_Dir(qf, kf, vf, lrf, cosf, sinf, wdf, bdf, cmf, of_ref, stf, False),
            _Dir(qb, kb, vb, lrb, cosb, sinb, wdb, bdb, cmb, ob_ref, stb, True))

    @pl.when(s == 0)
    def _():
        stf[...] = jnp.zeros_like(stf)
        stb[...] = jnp.zeros_like(stb)

    @pl.when(s < n_ctx_blk)
    def _():
        _gla_block(dirs, False, **kw)

    @pl.when(s >= n_ctx_blk)
    def _():
        _gla_block(dirs, True, **kw)


def _gla(z, lr, cos_t, sin_t, wdf, bdf, wdb, bdb, cmf, cmb, *, bsz, s_len, c_len, tb, hb, dk, dv):
    n_ctx_blk, n_x_blk = c_len // tb, s_len // tb
    nb = n_ctx_blk + n_x_blk
    key_w = GLA_HEADS * dk
    qw, vw = hb * dk, hb * dv
    k_off, v_off = key_w // qw, (2 * key_w) // vw

    def fblk(s):
        return s

    def bblk(s):
        return jnp.where(s < n_ctx_blk, n_ctx_blk - 1 - s, nb - 1 - s + n_ctx_blk)

    def specs(blk):
        return [
            pl.BlockSpec((tb, qw), lambda b, g, s: (b * nb + blk(s), g)),
            pl.BlockSpec((tb, qw), lambda b, g, s: (b * nb + blk(s), k_off + g)),
            pl.BlockSpec((tb, vw), lambda b, g, s: (b * nb + blk(s), v_off + g)),
            pl.BlockSpec((tb, V7X_LANES), lambda b, g, s: (b * nb + blk(s), 0)),
            pl.BlockSpec((tb, qw), lambda b, g, s: (blk(s), 0)),
            pl.BlockSpec((tb, qw), lambda b, g, s: (blk(s), 0)),
        ]

    def wspec():
        return [pl.BlockSpec((V7X_LANES, qw), lambda b, g, s: (0, g)),
                pl.BlockSpec((1, qw), lambda b, g, s: (0, g))]

    def oblk_f(b, g, s):
        return (b * n_x_blk + jnp.maximum(s - n_ctx_blk, 0), g)

    def oblk_b(b, g, s):
        return (b * n_x_blk + jnp.minimum(nb - 1 - s + n_ctx_blk, nb - 1) - n_ctx_blk, g)

    cspec = pl.BlockSpec((2 * tb, tb), lambda b, g, s: (0, 0))
    kern = functools.partial(_gla_kernel, n_ctx_blk=n_ctx_blk, hb=hb, dk=dk, dv=dv, tb=tb)
    o_shape = jax.ShapeDtypeStruct((bsz * s_len, GLA_HEADS * dv), BF16)
    return pl.pallas_call(
        kern,
        grid=(bsz, GLA_HEADS // hb, nb),
        in_specs=specs(fblk) + specs(bblk) + wspec() + wspec() + [cspec, cspec],
        out_specs=[pl.BlockSpec((tb, vw), oblk_f), pl.BlockSpec((tb, vw), oblk_b)],
        out_shape=[o_shape, o_shape],
        scratch_shapes=[pltpu.VMEM((hb, dv, dk), F32), pltpu.VMEM((hb, dv, dk), F32)],
        compiler_params=_params(("arbitrary", "arbitrary", "arbitrary")),
        name="gla_scan",
    )(z, z, z, lr, cos_t, sin_t, z, z, z, lr, cos_t, sin_t, wdf, bdf, wdb, bdb, cmf, cmb)


def _mix_kernel(of_ref, ob_ref, r_ref, u_ref, vv_ref, gn_ref, lg_ref, lb_ref, ws_ref, bs_ref, y_ref,
                *, dv, val_w, groups, tb):
    o = of_ref[...].astype(F32) + ob_ref[...].astype(F32)
    for h in range(val_w // dv):
        cs = slice(h * dv, (h + 1) * dv)
        r = r_ref[:, cs].astype(F32)
        y_ref[:, cs] = (_rms(o[:, cs], gn_ref[:, cs]) * _silu(r)).astype(y_ref.dtype)

    u = u_ref[...].astype(F32)
    vv = vv_ref[...].astype(F32)
    mu = jnp.mean(vv, axis=-1, keepdims=True)
    cen = vv - mu
    var = jnp.mean(cen * cen, axis=-1, keepdims=True)
    vn = (cen * lax.rsqrt(var + EPS) * lg_ref[...] + lb_ref[...]).astype(BF16)
    sgw = vn.shape[1]
    gw = sgw // groups
    for c in range(tb // SG_CHUNK):
        rows = slice(c * SG_CHUNK, (c + 1) * SG_CHUNK)
        for g in range(groups):
            cs = slice(g * gw, (g + 1) * gw)
            sg = jnp.dot(ws_ref[g], vn[rows, cs], preferred_element_type=F32) + bs_ref[:, cs]
            y_ref[rows, val_w + g * gw:val_w + (g + 1) * gw] = (u[rows, cs] * sg).astype(y_ref.dtype)


def _mix(o_f, o_b, z, zg, gn, lg, lb, ws, bsx, *, bsz, s_len, c_len, tb, dv, val_w, sgw, r_off):
    t = bsz * s_len
    n_x, nb = s_len // tb, (s_len + c_len) // tb
    groups = ws.shape[0]

    def zrow(i):
        return (i // n_x) * nb + c_len // tb + i % n_x

    kern = functools.partial(_mix_kernel, dv=dv, val_w=val_w, groups=groups, tb=tb)
    return pl.pallas_call(
        kern,
        grid=(t // tb,),
        in_specs=[pl.BlockSpec((tb, val_w), lambda i: (i, 0)),
                  pl.BlockSpec((tb, val_w), lambda i: (i, 0)),
                  pl.BlockSpec((tb, val_w), lambda i: (zrow(i), r_off // val_w)),
                  pl.BlockSpec((tb, sgw), lambda i: (zrow(i), 0)),
                  pl.BlockSpec((tb, sgw), lambda i: (zrow(i), 1)),
                  pl.BlockSpec((1, val_w), lambda i: (0, 0)),
                  pl.BlockSpec((1, sgw), lambda i: (0, 0)),
                  pl.BlockSpec((1, sgw), lambda i: (0, 0)),
                  pl.BlockSpec((groups, SG_CHUNK, SG_CHUNK), lambda i: (0, 0, 0)),
                  pl.BlockSpec((SG_CHUNK, sgw), lambda i: (0, 0))],
        out_specs=pl.BlockSpec((tb, val_w + sgw), lambda i: (i, 0)),
        out_shape=jax.ShapeDtypeStruct((t, val_w + sgw), BF16),
        compiler_params=_params(("arbitrary",)),
        name="mix_readout",
    )(o_f, o_b, z, zg, zg, gn, lg, lb, ws, bsx)


def _res1_kernel(x_ref, m_ref, g1_ref, sh_ref, sc_ref, pg_ref, ng_ref, x1_ref, h2_ref):
    x1 = x_ref[...] + g1_ref[...] * _rms(m_ref[...].astype(F32), pg_ref[...])
    x1_ref[...] = x1
    h2_ref[...] = (_rms(x1, ng_ref[...]) * (1.0 + sc_ref[...]) + sh_ref[...]).astype(h2_ref.dtype)


def _res1(x2, mix, mod3, post_g, pre_g, *, s_len, tb):
    t, d = x2.shape
    n_x = s_len // tb
    row = pl.BlockSpec((tb, d), lambda i: (i, 0))
    vec = pl.BlockSpec((1, d), lambda i: (0, 0))

    def modc(col):
        return pl.BlockSpec((None, 1, d), lambda i: (i // n_x, 0, col))

    return pl.pallas_call(
        _res1_kernel,
        grid=(t // tb,),
        in_specs=[row, row, modc(2), modc(3), modc(4), vec, vec],
        out_specs=[row, row],
        out_shape=[jax.ShapeDtypeStruct((t, d), F32), jax.ShapeDtypeStruct((t, d), BF16)],
        compiler_params=_params(("arbitrary",)),
        name="residual1",
    )(x2, mix, mod3, mod3, mod3, post_g, pre_g)


def _res2_kernel(x_ref, m_ref, g2_ref, pg_ref, o_ref):
    o_ref[...] = x_ref[...] + g2_ref[...] * _rms(m_ref[...].astype(F32), pg_ref[...])


def _res2(x1, m2, mod3, post_g, *, s_len, tb):
    t, d = x1.shape
    n_x = s_len // tb
    row = pl.BlockSpec((tb, d), lambda i: (i, 0))
    return pl.pallas_call(
        _res2_kernel,
        grid=(t // tb,),
        in_specs=[row, row, pl.BlockSpec((None, 1, d), lambda i: (i // n_x, 0, 5)),
                  pl.BlockSpec((1, d), lambda i: (0, 0))],
        out_specs=row,
        out_shape=jax.ShapeDtypeStruct((t, d), F32),
        compiler_params=_params(("arbitrary",)),
        name="residual2",
    )(x1, m2, mod3, post_g)


def _rope_tables(s_len, c_len, dk, hb):
    m = dk // 4
    inv_freq = ROPE_BASE ** (-jnp.arange(m, dtype=F32) / m)
    pos = jnp.arange(s_len)
    ang_r = (pos // GRID_W).astype(F32)[:, None] * inv_freq[None, :]
    ang_c = (pos % GRID_W).astype(F32)[:, None] * inv_freq[None, :]
    cos = jnp.concatenate([jnp.cos(ang_r)] * 2 + [jnp.cos(ang_c)] * 2, axis=-1)
    sin = jnp.concatenate([-jnp.sin(ang_r), jnp.sin(ang_r), -jnp.sin(ang_c), jnp.sin(ang_c)], axis=-1)
    cos = jnp.concatenate([jnp.ones((c_len, dk), F32), cos], axis=0)
    sin = jnp.concatenate([jnp.zeros((c_len, dk), F32), sin], axis=0)
    return jnp.tile(cos, (1, hb)), jnp.tile(sin, (1, hb))


def _chunk_matrices(tb):
    i = jnp.arange(tb)
    same = (i[:, None] // GLA_CHUNK) == (i[None, :] // GLA_CHUNK)
    fwd = same & (i[None, :] <= i[:, None])
    bwd = same & (i[None, :] >= i[:, None])
    return (jnp.concatenate([fwd, same], axis=0).astype(BF16),
            jnp.concatenate([bwd, same], axis=0).astype(BF16))


def _tile(n, pref):
    return pref if n % pref == 0 else n


def kernel(x, c, ctx, c_ctx, w_ada, b_ada, pre1_g, post1_g, pre2_g, post2_g, w_in, w_dec_f, b_dec_f,
           w_dec_b, b_dec_b, gla_norm_g, sg_ln_g, sg_ln_b, w_s, b_s, w_o, w_1, w_2):
    bsz, s_len, d = x.shape
    c_len = ctx.shape[1]
    assert w_ada.shape[0] == 1, "single-layer block only"
    assert bsz < MOD_ROWS
    lowrank, key_w = w_dec_f.shape[1], w_dec_f.shape[2]
    dk = key_w // GLA_HEADS
    dv = gla_norm_g.shape[2]
    val_w = GLA_HEADS * dv
    sgw = sg_ln_g.shape[1]
    lf0 = 2 * key_w + 2 * val_w
    sg0 = lf0 + 2 * lowrank
    assert dk == V7X_LANES and 2 * lowrank <= V7X_LANES and w_in.shape[2] == sg0 + 2 * sgw
    t = bsz * s_len

    w_in0 = w_in[0]
    w_gla = w_in0[:, :lf0].astype(BF16)
    w_sg = w_in0[:, sg0:].astype(BF16)
    w_lr = jnp.pad(w_in0[:, lf0:sg0], ((0, 0), (0, V7X_LANES - 2 * lowrank))).astype(BF16)
    w_o_b, w_1_b, w_2_b = w_o[0].astype(BF16), w_1[0].astype(BF16), w_2[0].astype(BF16)
    wdf = jnp.pad(w_dec_f[0], ((0, V7X_LANES - lowrank), (0, 0)))
    wdb = jnp.pad(w_dec_b[0], ((lowrank, V7X_LANES - 2 * lowrank), (0, 0)))

    c_all = jnp.concatenate([c, c_ctx[None], jnp.zeros((MOD_ROWS - bsz - 1, d), F32)], axis=0)
    mod = _ada(c_all, w_ada[0], b_ada[0])
    mod3 = mod.reshape(MOD_ROWS, 1, N_MOD * d)

    rb = _tile(math.gcd(s_len, c_len), 256)
    hx = _prenorm(x, ctx, mod3, pre1_g, rb)

    m_all = hx.shape[0]
    bm_all = _tile(m_all, 1024)
    z, lr = _inproj(hx, w_gla, w_lr, bm_all, _tile(lf0, 1024))
    zg = _matmul(hx, w_sg, BF16, bm_all, _tile(2 * sgw, 1024), d, act="gelu", name="in_proj_sg")

    hb = min(GLA_HEADS, 4)
    cos_t, sin_t = _rope_tables(s_len, c_len, dk, hb)
    cmf, cmb = _chunk_matrices(rb)
    o_f, o_b = _gla(z, lr, cos_t, sin_t, wdf, b_dec_f, wdb, b_dec_b, cmf, cmb,
                    bsz=bsz, s_len=s_len, c_len=c_len, tb=rb, hb=hb, dk=dk, dv=dv)

    bsx = jnp.repeat(b_s[0].T, sgw // w_s.shape[1], axis=1)
    y = _mix(o_f, o_b, z, zg, gla_norm_g.reshape(1, val_w), sg_ln_g, sg_ln_b, w_s[0].astype(BF16), bsx,
             bsz=bsz, s_len=s_len, c_len=c_len, tb=rb, dv=dv, val_w=val_w, sgw=sgw, r_off=lf0 - val_w)

    bm = _tile(t, 1024)
    mix = _matmul(y, w_o_b, BF16, bm, _tile(d, 1024), d, name="out_proj")
    x2 = x.reshape(t, d)
    x1, h2 = _res1(x2, mix, mod3, post1_g, pre2_g, s_len=s_len, tb=rb)
    d_ff = w_1_b.shape[1]
    hmid = _matmul(h2, w_1_b, BF16, bm, _tile(d_ff, 1024), d, act="relu2", name="mlp_up")
    m2 = _matmul(hmid, w_2_b, BF16, bm, _tile(d, 1024), _tile(d_ff, 4096), name="mlp_down")
    out = _res2(x1, m2, mod3, post2_g, s_len=s_len, tb=rb)
    return out.reshape(bsz, s_len, d)
```

```python
import collections
import functools
import math

import jax
import jax.numpy as jnp
from jax import lax
from jax.experimental import pallas as pl
from jax.experimental.pallas import tpu as pltpu

GRID_W = 64
GLA_HEADS = 8
GLA_TAU = 16.0
ROPE_BASE = 10000.0
N_MOD = 6
EPS = 1e-6
SG_CHUNK = 128
GLA_CHUNK = 64
LOG2_E = math.log2(math.e)

V7X_LANES = 128
MOD_ROWS = 8
VMEM_LIMIT_BYTES = 56 * 1024 * 1024

F32 = jnp.float32
BF16 = jnp.bfloat16


def _params(sem, vmem=VMEM_LIMIT_BYTES):
    return pltpu.CompilerParams(dimension_semantics=sem, vmem_limit_bytes=vmem)


def _rms(t, g):
    return t * lax.rsqrt(jnp.mean(t * t, axis=-1, keepdims=True) + EPS) * g


def _silu(t):
    return t * (1.0 / (1.0 + jnp.exp(-t)))


def _gelu(t):
    return 0.5 * t * (1.0 + lax.erf(t * (2.0 ** -0.5)))


def _ada_kernel(c_ref, w_ref, b_ref, o_ref):
    cond = _silu(c_ref[...])
    o_ref[...] = jnp.dot(cond.astype(BF16), w_ref[...].astype(BF16),
                         preferred_element_type=F32) + b_ref[...]


def _ada(c_all, w_ada, b_ada):
    d, n = w_ada.shape
    bn = min(n, 1024)
    return pl.pallas_call(
        _ada_kernel,
        grid=(n // bn,),
        in_specs=[pl.BlockSpec((MOD_ROWS, d), lambda j: (0, 0)),
                  pl.BlockSpec((d, bn), lambda j: (0, j)),
                  pl.BlockSpec((1, bn), lambda j: (0, j))],
        out_specs=pl.BlockSpec((MOD_ROWS, bn), lambda j: (0, j)),
        out_shape=jax.ShapeDtypeStruct((MOD_ROWS, n), F32),
        compiler_params=_params(("arbitrary",)),
        name="ada_mod",
    )(c_all, w_ada, b_ada.reshape(1, n))


def _wsplit_kernel(w_ref, gla_ref, sg_ref, lr_ref, *, lf0, sg0):
    gla_ref[...] = w_ref[:, :lf0].astype(gla_ref.dtype)
    sg_ref[...] = w_ref[:, sg0:].astype(sg_ref.dtype)
    lr_ref[...] = jnp.zeros_like(lr_ref)
    lr_ref[:, :sg0 - lf0] = w_ref[:, lf0:sg0].astype(lr_ref.dtype)


def _wsplit(w_in, lf0, sg0):
    k, n = w_in.shape
    rb = _tile(k, 256)
    return pl.pallas_call(
        functools.partial(_wsplit_kernel, lf0=lf0, sg0=sg0),
        grid=(k // rb,),
        in_specs=[pl.BlockSpec((rb, n), lambda i: (i, 0))],
        out_specs=[pl.BlockSpec((rb, lf0), lambda i: (i, 0)),
                   pl.BlockSpec((rb, n - sg0), lambda i: (i, 0)),
                   pl.BlockSpec((rb, V7X_LANES), lambda i: (i, 0))],
        out_shape=[jax.ShapeDtypeStruct((k, lf0), BF16),
                   jax.ShapeDtypeStruct((k, n - sg0), BF16),
                   jax.ShapeDtypeStruct((k, V7X_LANES), BF16)],
        compiler_params=_params(("arbitrary",)),
        name="w_in_split",
    )(w_in)


def _prenorm_kernel(x_ref, ctx_ref, mod_ref, g_ref, o_ref, *, n_ctx_blk, d):
    j = pl.program_id(1)

    def emit(t):
        y = _rms(t, g_ref[...])
        o_ref[...] = (y * (1.0 + mod_ref[:, d:2 * d]) + mod_ref[:, 0:d]).astype(o_ref.dtype)

    @pl.when(j < n_ctx_blk)
    def _():
        emit(ctx_ref[...])

    @pl.when(j >= n_ctx_blk)
    def _():
        emit(x_ref[...])


def _prenorm(x, ctx, mod3, g, rb):
    bsz, s, d = x.shape
    c = ctx.shape[1]
    n_ctx_blk, nb = c // rb, (c + s) // rb
    return pl.pallas_call(
        functools.partial(_prenorm_kernel, n_ctx_blk=n_ctx_blk, d=d),
        grid=(bsz, nb),
        in_specs=[
            pl.BlockSpec((None, rb, d), lambda b, j: (b, jnp.maximum(j - n_ctx_blk, 0), 0)),
            pl.BlockSpec((None, rb, d), lambda b, j: (b, jnp.minimum(j, n_ctx_blk - 1), 0)),
            pl.BlockSpec((None, 1, 2 * d), lambda b, j: (jnp.where(j < n_ctx_blk, bsz, b), 0, 0)),
            pl.BlockSpec((1, d), lambda b, j: (0, 0)),
        ],
        out_specs=pl.BlockSpec((rb, d), lambda b, j: (b * nb + j, 0)),
        out_shape=jax.ShapeDtypeStruct((bsz * (c + s), d), BF16),
        compiler_params=_params(("arbitrary", "arbitrary")),
        name="prenorm_mod",
    )(x, ctx, mod3, g)


CAST_ROWS = 64


def _cast_specs(src, nsteps, lin):
    rows, cols = src.shape
    cr = CAST_ROWS
    while rows % cr or rows // cr > nsteps:
        cr += CAST_ROWS
    nblk = rows // cr
    spec = pl.BlockSpec((cr, cols), lambda *g: (jnp.minimum(lin(*g), nblk - 1), 0))
    return spec, spec, jax.ShapeDtypeStruct((rows, cols), BF16)


def _inproj_kernel(a_ref, w_ref, wlr_ref, src_ref, z_ref, lr_ref, dst_ref):
    a = a_ref[...]
    z_ref[...] = jnp.dot(a, w_ref[...], preferred_element_type=F32).astype(z_ref.dtype)
    dst_ref[...] = src_ref[...].astype(dst_ref.dtype)

    @pl.when(pl.program_id(1) == 0)
    def _():
        lr_ref[...] = jnp.dot(a, wlr_ref[...], preferred_element_type=F32)


def _inproj(a, w, wlr, cast_src, bm, bn):
    m, k = a.shape
    n = w.shape[1]
    ni, nj = m // bm, n // bn
    c_in, c_out, c_shape = _cast_specs(cast_src, ni * nj, lambda i, j: i * nj + j)
    return pl.pallas_call(
        _inproj_kernel,
        grid=(ni, nj),
        in_specs=[pl.BlockSpec((bm, k), lambda i, j: (i, 0)),
                  pl.BlockSpec((k, bn), lambda i, j: (0, j)),
                  pl.BlockSpec((k, V7X_LANES), lambda i, j: (0, 0)),
                  c_in],
        out_specs=[pl.BlockSpec((bm, bn), lambda i, j: (i, j)),
                   pl.BlockSpec((bm, V7X_LANES), lambda i, j: (i, 0)),
                   c_out],
        out_shape=[jax.ShapeDtypeStruct((m, n), BF16),
                   jax.ShapeDtypeStruct((m, V7X_LANES), F32),
                   c_shape],
        compiler_params=_params(("arbitrary", "arbitrary")),
        name="in_proj",
    )(a, w, wlr, cast_src)


def _matmul_kernel(a_ref, w_ref, *rest, nk, act, cast):
    if cast:
        src_ref, o_ref, dst_ref = rest[:3]
        dst_ref[...] = src_ref[...].astype(dst_ref.dtype)
    else:
        o_ref = rest[0]

    def finish(r):
        if act == "relu2":
            r = jnp.square(jnp.maximum(r, 0.0))
        elif act == "gelu":
            r = _gelu(r)
        o_ref[...] = r.astype(o_ref.dtype)

    if nk == 1:
        finish(jnp.dot(a_ref[...], w_ref[...], preferred_element_type=F32))
        return
    acc_ref = rest[-1]
    kk = pl.program_id(2)

    @pl.when(kk == 0)
    def _():
        acc_ref[...] = jnp.zeros_like(acc_ref)

    acc_ref[...] += jnp.dot(a_ref[...], w_ref[...], preferred_element_type=F32)

    @pl.when(kk == nk - 1)
    def _():
        finish(acc_ref[...])


def _matmul(a, w, out_dtype, bm, bn, bk, act=None, cast_src=None, name="matmul"):
    m, k = a.shape
    n = w.shape[1]
    ni, nj, nk = m // bm, n // bn, k // bk
    in_specs = [pl.BlockSpec((bm, bk), lambda i, j, kk: (i, kk)),
                pl.BlockSpec((bk, bn), lambda i, j, kk: (kk, j))]
    out_specs = [pl.BlockSpec((bm, bn), lambda i, j, kk: (i, j))]
    out_shape = [jax.ShapeDtypeStruct((m, n), out_dtype)]
    args = [a, w]
    if cast_src is not None:
        c_in, c_out, c_shape = _cast_specs(cast_src, ni * nj * nk, lambda i, j, kk: (i * nj + j) * nk + kk)
        in_specs.append(c_in)
        out_specs.append(c_out)
        out_shape.append(c_shape)
        args.append(cast_src)
    res = pl.pallas_call(
        functools.partial(_matmul_kernel, nk=nk, act=act, cast=cast_src is not None),
        grid=(ni, nj, nk),
        in_specs=in_specs,
        out_specs=out_specs,
        out_shape=out_shape,
        scratch_shapes=[pltpu.VMEM((bm, bn), F32)] if nk > 1 else [],
        compiler_params=_params(("arbitrary", "arbitrary", "arbitrary")),
        name=name,
    )(*args)
    return res if cast_src is not None else res[0]


_Dir = collections.namedtuple("_Dir", "q k v lr cos sin wd bd cmat o st reverse")


def _gla_block(dirs, emit, *, hb, dk, dv, tb):
    nch = tb // GLA_CHUNK
    w = hb * dk
    lane = lax.broadcasted_iota(jnp.int32, (tb, w), 1)
    first = (lane & (dk // 4)) == 0
    nt = (((1,), (1,)), ((), ()))
    tn = (((0,), (0,)), ((), ()))

    def ks(h):
        return slice(h * dk, (h + 1) * dk)

    def vs(h):
        return slice(h * dv, (h + 1) * dv)

    def rows(c):
        return slice(c * GLA_CHUNK, (c + 1) * GLA_CHUNK)

    def rope(t, cos, sin):
        sw = jnp.where(first, pltpu.roll(t, w - dk // 4, 1), pltpu.roll(t, dk // 4, 1))
        return t * cos + sw * sin

    gates = []
    for d in dirs:
        a = jnp.dot(d.lr[...].astype(BF16), d.wd[...].astype(BF16), preferred_element_type=F32) + d.bd[...]
        la = (jnp.minimum(a, 0.0) - jnp.log1p(jnp.exp(-jnp.abs(a)))) * (LOG2_E / GLA_TAU)
        hi = la.astype(BF16)
        lo = (la - hi.astype(F32)).astype(BF16)
        tri = d.cmat[...]
        gates.append((jnp.dot(tri, hi, preferred_element_type=F32)
                      + jnp.dot(tri, lo, preferred_element_type=F32), tri))

    work = []
    for d, (cum, tri) in zip(dirs, gates):
        cos, sin = d.cos[...], d.sin[...]
        kr = rope(d.k[...].astype(F32), cos, sin)
        ends = [c * GLA_CHUNK if d.reverse else (c + 1) * GLA_CHUNK - 1 for c in range(nch)]
        tot = [cum[e:e + 1] for e in ends]
        kd = [(kr[rows(c)] * jnp.exp2(tot[c] - cum[rows(c)])).astype(BF16) for c in range(nch)]
        qe = ke = None
        if emit:
            qe = (rope(d.q[...].astype(F32), cos, sin) * (jnp.exp2(cum) * dk ** -0.5)).astype(BF16)
            ke = (kr * jnp.exp2(-cum)).astype(BF16)
        work.append((d, d.v[...], qe, ke, kd, tot, tri))

    heads = range(hb)
    intra = {}
    if emit:
        att = {(i, h): lax.dot_general(qe[:, ks(h)], ke[:, ks(h)], nt, preferred_element_type=F32)
               for i, (_, _, qe, ke, _, _, _) in enumerate(work) for h in heads}
        for i, (_, v, _, _, _, _, tri) in enumerate(work):
            for h in heads:
                att_m = jnp.where(tri > 0, att[i, h].astype(BF16), jnp.zeros((), BF16))
                intra[i, h] = jnp.dot(att_m, v[:, vs(h)], preferred_element_type=F32)

    upd = {(i, h, c): lax.dot_general(v[rows(c), vs(h)], kd[c][:, ks(h)], tn, preferred_element_type=F32)
           for i, (_, v, _, _, kd, _, _) in enumerate(work) for h in heads for c in range(nch)}

    seen = {}
    for i, (d, _, _, _, _, tot, _) in enumerate(work):
        order = range(nch - 1, -1, -1) if d.reverse else range(nch)
        for h in heads:
            st = d.st[h]
            for c in order:
                if emit:
                    seen[i, h, c] = st.astype(BF16)
                st = st * jnp.exp2(tot[c][:, ks(h)]) + upd[i, h, c]
            d.st[h] = st

    if emit:
        for i, (d, _, qe, _, _, _, _) in enumerate(work):
            for h in heads:
                for c in range(nch):
                    inter = lax.dot_general(qe[rows(c), ks(h)], seen[i, h, c], nt, preferred_element_type=F32)
                    d.o[rows(c), vs(h)] = (intra[i, h][rows(c)] + inter).astype(d.o.dtype)


def _gla_kernel(qf, kf, vf, lrf, cosf, sinf, qb, kb, vb, lrb, cosb, sinb,
                wdf, bdf, wdb, bdb, cmf, cmb, of_ref, ob_ref, stf, stb, *, n_ctx_blk, **kw):
    s = pl.program_id(2)
    dirs = (_Dir(qf, kf, vf, lrf, cosf, sinf, wdf, bdf, cmf, of_ref, stf, False),
            _Dir(qb, kb, vb, lrb, cosb, sinb, wdb, bdb, cmb, ob_ref, stb, True))

    @pl.when(s == 0)
    def _():
        stf[...] = jnp.zeros_like(stf)
        stb[...] = jnp.zeros_like(stb)

    @pl.when(s < n_ctx_blk)
    def _():
        _gla_block(dirs, False, **kw)

    @pl.when(s >= n_ctx_blk)
    def _():
        _gla_block(dirs, True, **kw)


def _gla(z, lr, cos_t, sin_t, wdf, bdf, wdb, bdb, cmf, cmb, *, bsz, s_len, c_len, tb, hb, dk, dv):
    n_ctx_blk, n_x_blk = c_len // tb, s_len // tb
    nb = n_ctx_blk + n_x_blk
    key_w = GLA_HEADS * dk
    qw, vw = hb * dk, hb * dv
    k_off, v_off = key_w // qw, (2 * key_w) // vw

    def fblk(s):
        return s

    def bblk(s):
        return jnp.where(s < n_ctx_blk, n_ctx_blk - 1 - s, nb - 1 - s + n_ctx_blk)

    def specs(blk):
        return [
            pl.BlockSpec((tb, qw), lambda b, g, s: (b * nb + blk(s), g)),
            pl.BlockSpec((tb, qw), lambda b, g, s: (b * nb + blk(s), k_off + g)),
            pl.BlockSpec((tb, vw), lambda b, g, s: (b * nb + blk(s), v_off + g)),
            pl.BlockSpec((tb, V7X_LANES), lambda b, g, s: (b * nb + blk(s), 0)),
            pl.BlockSpec((tb, qw), lambda b, g, s: (blk(s), 0)),
            pl.BlockSpec((tb, qw), lambda b, g, s: (blk(s), 0)),
        ]

    def wspec():
        return [pl.BlockSpec((V7X_LANES, qw), lambda b, g, s: (0, g)),
                pl.BlockSpec((1, qw), lambda b, g, s: (0, g))]

    def oblk_f(b, g, s):
        return (b * n_x_blk + jnp.maximum(s - n_ctx_blk, 0), g)

    def oblk_b(b, g, s):
        return (b * n_x_blk + jnp.minimum(nb - 1 - s + n_ctx_blk, nb - 1) - n_ctx_blk, g)

    cspec = pl.BlockSpec((tb, tb), lambda b, g, s: (0, 0))
    kern = functools.partial(_gla_kernel, n_ctx_blk=n_ctx_blk, hb=hb, dk=dk, dv=dv, tb=tb)
    o_shape = jax.ShapeDtypeStruct((bsz * s_len, GLA_HEADS * dv), BF16)
    return pl.pallas_call(
        kern,
        grid=(bsz, GLA_HEADS // hb, nb),
        in_specs=specs(fblk) + specs(bblk) + wspec() + wspec() + [cspec, cspec],
        out_specs=[pl.BlockSpec((tb, vw), oblk_f), pl.BlockSpec((tb, vw), oblk_b)],
        out_shape=[o_shape, o_shape],
        scratch_shapes=[pltpu.VMEM((hb, dv, dk), F32), pltpu.VMEM((hb, dv, dk), F32)],
        compiler_params=_params(("arbitrary", "arbitrary", "arbitrary")),
        name="gla_scan",
    )(z, z, z, lr, cos_t, sin_t, z, z, z, lr, cos_t, sin_t, wdf, bdf, wdb, bdb, cmf, cmb)


def _mix_kernel(of_ref, ob_ref, r_ref, u_ref, vv_ref, gn_ref, lg_ref, lb_ref, ws_ref, bs_ref, y_ref,
                *, dv, val_w, groups, tb):
    o = of_ref[...].astype(F32) + ob_ref[...].astype(F32)
    for h in range(val_w // dv):
        cs = slice(h * dv, (h + 1) * dv)
        r = r_ref[:, cs].astype(F32)
        y_ref[:, cs] = (_rms(o[:, cs], gn_ref[:, cs]) * _silu(r)).astype(y_ref.dtype)

    u = u_ref[...].astype(F32)
    vv = vv_ref[...].astype(F32)
    mu = jnp.mean(vv, axis=-1, keepdims=True)
    cen = vv - mu
    var = jnp.mean(cen * cen, axis=-1, keepdims=True)
    vn = (cen * lax.rsqrt(var + EPS) * lg_ref[...] + lb_ref[...]).astype(BF16)
    sgw = vn.shape[1]
    gw = sgw // groups
    for c in range(tb // SG_CHUNK):
        rows = slice(c * SG_CHUNK, (c + 1) * SG_CHUNK)
        for g in range(groups):
            cs = slice(g * gw, (g + 1) * gw)
            sg = jnp.dot(ws_ref[g], vn[rows, cs], preferred_element_type=F32) + bs_ref[:, cs]
            y_ref[rows, val_w + g * gw:val_w + (g + 1) * gw] = (u[rows, cs] * sg).astype(y_ref.dtype)


def _mix(o_f, o_b, z, zg, gn, lg, lb, ws, bsx, *, bsz, s_len, c_len, tb, dv, val_w, sgw, r_off):
    t = bsz * s_len
    n_x, nb = s_len // tb, (s_len + c_len) // tb
    groups = ws.shape[0]

    def zrow(i):
        return (i // n_x) * nb + c_len // tb + i % n_x

    kern = functools.partial(_mix_kernel, dv=dv, val_w=val_w, groups=groups, tb=tb)
    return pl.pallas_call(
        kern,
        grid=(t // tb,),
        in_specs=[pl.BlockSpec((tb, val_w), lambda i: (i, 0)),
                  pl.BlockSpec((tb, val_w), lambda i: (i, 0)),
                  pl.BlockSpec((tb, val_w), lambda i: (zrow(i), r_off // val_w)),
                  pl.BlockSpec((tb, sgw), lambda i: (zrow(i), 0)),
                  pl.BlockSpec((tb, sgw), lambda i: (zrow(i), 1)),
                  pl.BlockSpec((1, val_w), lambda i: (0, 0)),
                  pl.BlockSpec((1, sgw), lambda i: (0, 0)),
                  pl.BlockSpec((1, sgw), lambda i: (0, 0)),
                  pl.BlockSpec((groups, SG_CHUNK, SG_CHUNK), lambda i: (0, 0, 0)),
                  pl.BlockSpec((SG_CHUNK, sgw), lambda i: (0, 0))],
        out_specs=pl.BlockSpec((tb, val_w + sgw), lambda i: (i, 0)),
        out_shape=jax.ShapeDtypeStruct((t, val_w + sgw), BF16),
        compiler_params=_params(("arbitrary",)),
        name="mix_readout",
    )(o_f, o_b, z, zg, zg, gn, lg, lb, ws, bsx)


def _res1_kernel(x_ref, m_ref, g1_ref, sh_ref, sc_ref, pg_ref, ng_ref, x1_ref, h2_ref):
    x1 = x_ref[...] + g1_ref[...] * _rms(m_ref[...].astype(F32), pg_ref[...])
    x1_ref[...] = x1
    h2_ref[...] = (_rms(x1, ng_ref[...]) * (1.0 + sc_ref[...]) + sh_ref[...]).astype(h2_ref.dtype)


def _res1(x2, mix, mod3, post_g, pre_g, *, s_len, tb):
    t, d = x2.shape
    n_x = s_len // tb
    row = pl.BlockSpec((tb, d), lambda i: (i, 0))
    vec = pl.BlockSpec((1, d), lambda i: (0, 0))

    def modc(col):
        return pl.BlockSpec((None, 1, d), lambda i: (i // n_x, 0, col))

    return pl.pallas_call(
        _res1_kernel,
        grid=(t // tb,),
        in_specs=[row, row, modc(2), modc(3), modc(4), vec, vec],
        out_specs=[row, row],
        out_shape=[jax.ShapeDtypeStruct((t, d), F32), jax.ShapeDtypeStruct((t, d), BF16)],
        compiler_params=_params(("arbitrary",)),
        name="residual1",
    )(x2, mix, mod3, mod3, mod3, post_g, pre_g)


def _res2_kernel(x_ref, m_ref, g2_ref, pg_ref, o_ref):
    o_ref[...] = x_ref[...] + g2_ref[...] * _rms(m_ref[...].astype(F32), pg_ref[...])


def _res2(x1, m2, mod3, post_g, *, s_len, tb):
    t, d = x1.shape
    n_x = s_len // tb
    row = pl.BlockSpec((tb, d), lambda i: (i, 0))
    return pl.pallas_call(
        _res2_kernel,
        grid=(t // tb,),
        in_specs=[row, row, pl.BlockSpec((None, 1, d), lambda i: (i // n_x, 0, 5)),
                  pl.BlockSpec((1, d), lambda i: (0, 0))],
        out_specs=row,
        out_shape=jax.ShapeDtypeStruct((t, d), F32),
        compiler_params=_params(("arbitrary",)),
        name="residual2",
    )(x1, m2, mod3, post_g)


def _rope_tables(s_len, c_len, dk, hb):
    m = dk // 4
    inv_freq = ROPE_BASE ** (-jnp.arange(m, dtype=F32) / m)
    pos = jnp.arange(s_len)
    ang_r = (pos // GRID_W).astype(F32)[:, None] * inv_freq[None, :]
    ang_c = (pos % GRID_W).astype(F32)[:, None] * inv_freq[None, :]
    cos = jnp.concatenate([jnp.cos(ang_r)] * 2 + [jnp.cos(ang_c)] * 2, axis=-1)
    sin = jnp.concatenate([-jnp.sin(ang_r), jnp.sin(ang_r), -jnp.sin(ang_c), jnp.sin(ang_c)], axis=-1)
    cos = jnp.concatenate([jnp.ones((c_len, dk), F32), cos], axis=0)
    sin = jnp.concatenate([jnp.zeros((c_len, dk), F32), sin], axis=0)
    return jnp.tile(cos, (1, hb)), jnp.tile(sin, (1, hb))


def _chunk_matrices(tb):
    i = jnp.arange(tb)
    same = (i[:, None] // GLA_CHUNK) == (i[None, :] // GLA_CHUNK)
    return ((same & (i[None, :] <= i[:, None])).astype(BF16),
            (same & (i[None, :] >= i[:, None])).astype(BF16))


def _tile(n, pref):
    return pref if n % pref == 0 else n


def kernel(x, c, ctx, c_ctx, w_ada, b_ada, pre1_g, post1_g, pre2_g, post2_g, w_in, w_dec_f, b_dec_f,
           w_dec_b, b_dec_b, gla_norm_g, sg_ln_g, sg_ln_b, w_s, b_s, w_o, w_1, w_2):
    bsz, s_len, d = x.shape
    c_len = ctx.shape[1]
    assert w_ada.shape[0] == 1, "single-layer block only"
    assert bsz < MOD_ROWS
    lowrank, key_w = w_dec_f.shape[1], w_dec_f.shape[2]
    dk = key_w // GLA_HEADS
    dv = gla_norm_g.shape[2]
    val_w = GLA_HEADS * dv
    sgw = sg_ln_g.shape[1]
    lf0 = 2 * key_w + 2 * val_w
    sg0 = lf0 + 2 * lowrank
    assert dk == V7X_LANES and 2 * lowrank <= V7X_LANES and w_in.shape[2] == sg0 + 2 * sgw
    t = bsz * s_len

    w_gla, w_sg, w_lr = _wsplit(w_in[0], lf0, sg0)
    wdf = jnp.pad(w_dec_f[0], ((0, V7X_LANES - lowrank), (0, 0)))
    wdb = jnp.pad(w_dec_b[0], ((lowrank, V7X_LANES - 2 * lowrank), (0, 0)))

    c_all = jnp.concatenate([c, c_ctx[None], jnp.zeros((MOD_ROWS - bsz - 1, d), F32)], axis=0)
    mod = _ada(c_all, w_ada[0], b_ada[0])
    mod3 = mod.reshape(MOD_ROWS, 1, N_MOD * d)

    rb = _tile(math.gcd(s_len, c_len), 256)
    hx = _prenorm(x, ctx, mod3, pre1_g, rb)

    m_all = hx.shape[0]
    bm_all = _tile(m_all, 1024)
    z, lr, w_o_b = _inproj(hx, w_gla, w_lr, w_o[0], bm_all, _tile(lf0, 1024))
    zg = _matmul(hx, w_sg, BF16, bm_all, _tile(2 * sgw, 1024), d, act="gelu", name="in_proj_sg")

    hb = min(GLA_HEADS, 4)
    cos_t, sin_t = _rope_tables(s_len, c_len, dk, hb)
    cmf, cmb = _chunk_matrices(rb)
    o_f, o_b = _gla(z, lr, cos_t, sin_t, wdf, b_dec_f, wdb, b_dec_b, cmf, cmb,
                    bsz=bsz, s_len=s_len, c_len=c_len, tb=rb, hb=hb, dk=dk, dv=dv)

    bsx = jnp.repeat(b_s[0].T, sgw // w_s.shape[1], axis=1)
    y = _mix(o_f, o_b, z, zg, gla_norm_g.reshape(1, val_w), sg_ln_g, sg_ln_b, w_s[0].astype(BF16), bsx,
             bsz=bsz, s_len=s_len, c_len=c_len, tb=rb, dv=dv, val_w=val_w, sgw=sgw, r_off=lf0 - val_w)

    bm = _tile(t, 1024)
    mix, w_1_b = _matmul(y, w_o_b, BF16, bm, _tile(d, 1024), d, cast_src=w_1[0], name="out_proj")
    x2 = x.reshape(t, d)
    x1, h2 = _res1(x2, mix, mod3, post1_g, pre2_g, s_len=s_len, tb=rb)
    d_ff = w_1_b.shape[1]
    hmid, w_2_b = _matmul(h2, w_1_b, BF16, bm, _tile(d_ff, 1024), d, act="relu2", cast_src=w_2[0],
                          name="mlp_up")
    m2 = _matmul(hmid, w_2_b, BF16, bm, _tile(d, 1024), _tile(d_ff, 4096), name="mlp_down")
    out = _res2(x1, m2, mod3, post2_g, s_len=s_len, tb=rb)
    return out.reshape(bsz, s_len, d)
```

```python
import collections
import functools
import math

import jax
import jax.numpy as jnp
from jax import lax
from jax.experimental import pallas as pl
from jax.experimental.pallas import tpu as pltpu

GRID_W = 64
GLA_HEADS = 8
GLA_TAU = 16.0
ROPE_BASE = 10000.0
N_MOD = 6
EPS = 1e-6
SG_CHUNK = 128
GLA_CHUNK = 64
LOG2_E = math.log2(math.e)

V7X_LANES = 128
MOD_ROWS = 8
VMEM_LIMIT_BYTES = 56 * 1024 * 1024

F32 = jnp.float32
BF16 = jnp.bfloat16


def _params(sem, vmem=VMEM_LIMIT_BYTES):
    return pltpu.CompilerParams(dimension_semantics=sem, vmem_limit_bytes=vmem)


def _rms(t, g):
    return t * lax.rsqrt(jnp.mean(t * t, axis=-1, keepdims=True) + EPS) * g


def _silu(t):
    return t * (1.0 / (1.0 + jnp.exp(-t)))


def _gelu(t):
    return 0.5 * t * (1.0 + lax.erf(t * (2.0 ** -0.5)))


def _ada_kernel(c_ref, w_ref, b_ref, o_ref):
    cond = _silu(c_ref[...])
    o_ref[...] = jnp.dot(cond.astype(BF16), w_ref[...].astype(BF16),
                         preferred_element_type=F32) + b_ref[...]


def _ada(c_all, w_ada, b_ada):
    d, n = w_ada.shape
    bn = min(n, 1024)
    return pl.pallas_call(
        _ada_kernel,
        grid=(n // bn,),
        in_specs=[pl.BlockSpec((MOD_ROWS, d), lambda j: (0, 0)),
                  pl.BlockSpec((d, bn), lambda j: (0, j)),
                  pl.BlockSpec((1, bn), lambda j: (0, j))],
        out_specs=pl.BlockSpec((MOD_ROWS, bn), lambda j: (0, j)),
        out_shape=jax.ShapeDtypeStruct((MOD_ROWS, n), F32),
        compiler_params=_params(("arbitrary",)),
        name="ada_mod",
    )(c_all, w_ada, b_ada.reshape(1, n))


def _prenorm_kernel(x_ref, ctx_ref, mod_ref, g_ref, o_ref, *, n_ctx_blk, d):
    j = pl.program_id(1)

    def emit(t):
        y = _rms(t, g_ref[...])
        o_ref[...] = (y * (1.0 + mod_ref[:, d:2 * d]) + mod_ref[:, 0:d]).astype(o_ref.dtype)

    @pl.when(j < n_ctx_blk)
    def _():
        emit(ctx_ref[...])

    @pl.when(j >= n_ctx_blk)
    def _():
        emit(x_ref[...])


def _prenorm(x, ctx, mod3, g, rb):
    bsz, s, d = x.shape
    c = ctx.shape[1]
    n_ctx_blk, nb = c // rb, (c + s) // rb
    return pl.pallas_call(
        functools.partial(_prenorm_kernel, n_ctx_blk=n_ctx_blk, d=d),
        grid=(bsz, nb),
        in_specs=[
            pl.BlockSpec((None, rb, d), lambda b, j: (b, jnp.maximum(j - n_ctx_blk, 0), 0)),
            pl.BlockSpec((None, rb, d), lambda b, j: (b, jnp.minimum(j, n_ctx_blk - 1), 0)),
            pl.BlockSpec((None, 1, 2 * d), lambda b, j: (jnp.where(j < n_ctx_blk, bsz, b), 0, 0)),
            pl.BlockSpec((1, d), lambda b, j: (0, 0)),
        ],
        out_specs=pl.BlockSpec((rb, d), lambda b, j: (b * nb + j, 0)),
        out_shape=jax.ShapeDtypeStruct((bsz * (c + s), d), BF16),
        compiler_params=_params(("arbitrary", "arbitrary")),
        name="prenorm_mod",
    )(x, ctx, mod3, g)


CAST_ROWS = 64


def _cast_specs(src, nsteps, lin):
    rows, cols = src.shape
    cr = CAST_ROWS
    while rows % cr or rows // cr > nsteps:
        cr += CAST_ROWS
    nblk = rows // cr
    spec = pl.BlockSpec((cr, cols), lambda *g: (jnp.minimum(lin(*g), nblk - 1), 0))
    return spec, spec, jax.ShapeDtypeStruct((rows, cols), BF16)


_NT = (((1,), (1,)), ((), ()))


def _inproj_kernel(a_ref, w_ref, wlr_ref, src_ref, z_ref, lr_ref, dst_ref):
    a = a_ref[...]
    z_ref[...] = lax.dot_general(a, w_ref[...], _NT, preferred_element_type=F32).astype(z_ref.dtype)
    dst_ref[...] = src_ref[...].astype(dst_ref.dtype)

    @pl.when(pl.program_id(1) == 0)
    def _():
        lr_ref[...] = lax.dot_general(a, wlr_ref[...], _NT, preferred_element_type=F32)


def _inproj(a, w_t, n, wlr_t, cast_src, bm, bn):
    m, k = a.shape
    ni, nj = m // bm, n // bn
    c_in, c_out, c_shape = _cast_specs(cast_src, ni * nj, lambda i, j: i * nj + j)
    return pl.pallas_call(
        _inproj_kernel,
        grid=(ni, nj),
        in_specs=[pl.BlockSpec((bm, k), lambda i, j: (i, 0)),
                  pl.BlockSpec((bn, k), lambda i, j: (j, 0)),
                  pl.BlockSpec((V7X_LANES, k), lambda i, j: (0, 0)),
                  c_in],
        out_specs=[pl.BlockSpec((bm, bn), lambda i, j: (i, j)),
                   pl.BlockSpec((bm, V7X_LANES), lambda i, j: (i, 0)),
                   c_out],
        out_shape=[jax.ShapeDtypeStruct((m, n), BF16),
                   jax.ShapeDtypeStruct((m, V7X_LANES), F32),
                   c_shape],
        compiler_params=_params(("arbitrary", "arbitrary")),
        name="in_proj",
    )(a, w_t, wlr_t, cast_src)


def _matmul_kernel(a_ref, w_ref, *rest, nk, act, cast, w_t):
    if cast:
        src_ref, o_ref, dst_ref = rest[:3]
        dst_ref[...] = src_ref[...].astype(dst_ref.dtype)
    else:
        o_ref = rest[0]

    def product():
        if w_t:
            return lax.dot_general(a_ref[...], w_ref[...], _NT, preferred_element_type=F32)
        return jnp.dot(a_ref[...], w_ref[...], preferred_element_type=F32)

    def finish(r):
        if act == "relu2":
            r = jnp.square(jnp.maximum(r, 0.0))
        elif act == "gelu":
            r = _gelu(r)
        o_ref[...] = r.astype(o_ref.dtype)

    if nk == 1:
        finish(product())
        return
    acc_ref = rest[-1]
    kk = pl.program_id(2)

    @pl.when(kk == 0)
    def _():
        acc_ref[...] = jnp.zeros_like(acc_ref)

    acc_ref[...] += product()

    @pl.when(kk == nk - 1)
    def _():
        finish(acc_ref[...])


def _matmul(a, w, out_dtype, bm, bn, bk, act=None, cast_src=None, w_t=False, name="matmul"):
    m, k = a.shape
    n = w.shape[0] if w_t else w.shape[1]
    ni, nj, nk = m // bm, n // bn, k // bk
    w_spec = (pl.BlockSpec((bn, bk), lambda i, j, kk: (j, kk)) if w_t
              else pl.BlockSpec((bk, bn), lambda i, j, kk: (kk, j)))
    in_specs = [pl.BlockSpec((bm, bk), lambda i, j, kk: (i, kk)), w_spec]
    out_specs = [pl.BlockSpec((bm, bn), lambda i, j, kk: (i, j))]
    out_shape = [jax.ShapeDtypeStruct((m, n), out_dtype)]
    args = [a, w]
    if cast_src is not None:
        c_in, c_out, c_shape = _cast_specs(cast_src, ni * nj * nk, lambda i, j, kk: (i * nj + j) * nk + kk)
        in_specs.append(c_in)
        out_specs.append(c_out)
        out_shape.append(c_shape)
        args.append(cast_src)
    res = pl.pallas_call(
        functools.partial(_matmul_kernel, nk=nk, act=act, cast=cast_src is not None, w_t=w_t),
        grid=(ni, nj, nk),
        in_specs=in_specs,
        out_specs=out_specs,
        out_shape=out_shape,
        scratch_shapes=[pltpu.VMEM((bm, bn), F32)] if nk > 1 else [],
        compiler_params=_params(("arbitrary", "arbitrary", "arbitrary")),
        name=name,
    )(*args)
    return res if cast_src is not None else res[0]


_Dir = collections.namedtuple("_Dir", "q k v lr cos sin wd bd cmat o st reverse")


def _gla_block(dirs, emit, *, hb, dk, dv, tb):
    nch = tb // GLA_CHUNK
    w = hb * dk
    lane = lax.broadcasted_iota(jnp.int32, (tb, w), 1)
    first = (lane & (dk // 4)) == 0
    nt = (((1,), (1,)), ((), ()))
    tn = (((0,), (0,)), ((), ()))

    def ks(h):
        return slice(h * dk, (h + 1) * dk)

    def vs(h):
        return slice(h * dv, (h + 1) * dv)

    def rows(c):
        return slice(c * GLA_CHUNK, (c + 1) * GLA_CHUNK)

    def rope(t, cos, sin):
        sw = jnp.where(first, pltpu.roll(t, w - dk // 4, 1), pltpu.roll(t, dk // 4, 1))
        return t * cos + sw * sin

    gates = []
    for d in dirs:
        a = jnp.dot(d.lr[...].astype(BF16), d.wd[...].astype(BF16), preferred_element_type=F32) + d.bd[...]
        la = (jnp.minimum(a, 0.0) - jnp.log(1.0 + jnp.exp(-jnp.abs(a)))) * (LOG2_E / GLA_TAU)
        hi = la.astype(BF16)
        lo = (la - hi.astype(F32)).astype(BF16)
        tri = d.cmat[...]
        gates.append((jnp.dot(tri, hi, preferred_element_type=F32)
                      + jnp.dot(tri, lo, preferred_element_type=F32), tri))

    work = []
    for d, (cum, tri) in zip(dirs, gates):
        cos, sin = d.cos[...], d.sin[...]
        kr = rope(d.k[...].astype(F32), cos, sin)
        ends = [c * GLA_CHUNK if d.reverse else (c + 1) * GLA_CHUNK - 1 for c in range(nch)]
        tot = [cum[e:e + 1] for e in ends]
        kd = [(kr[rows(c)] * jnp.exp2(tot[c] - cum[rows(c)])).astype(BF16) for c in range(nch)]
        qe = ke = None
        if emit:
            qe = (rope(d.q[...].astype(F32), cos, sin) * (jnp.exp2(cum) * dk ** -0.5)).astype(BF16)
            ke = (kr * jnp.exp2(-cum)).astype(BF16)
        work.append((d, d.v[...], qe, ke, kd, tot, tri))

    heads = range(hb)
    intra = {}
    if emit:
        att = {(i, h): lax.dot_general(qe[:, ks(h)], ke[:, ks(h)], nt, preferred_element_type=F32)
               for i, (_, _, qe, ke, _, _, _) in enumerate(work) for h in heads}
        for i, (_, v, _, _, _, _, tri) in enumerate(work):
            for h in heads:
                att_m = jnp.where(tri > 0, att[i, h].astype(BF16), jnp.zeros((), BF16))
                intra[i, h] = jnp.dot(att_m, v[:, vs(h)], preferred_element_type=F32)

    upd = {(i, h, c): lax.dot_general(v[rows(c), vs(h)], kd[c][:, ks(h)], tn, preferred_element_type=F32)
           for i, (_, v, _, _, kd, _, _) in enumerate(work) for h in heads for c in range(nch)}

    seen = {}
    for i, (d, _, _, _, _, tot, _) in enumerate(work):
        order = range(nch - 1, -1, -1) if d.reverse else range(nch)
        for h in heads:
            st = d.st[h]
            for c in order:
                if emit:
                    seen[i, h, c] = st.astype(BF16)
                st = st * jnp.exp2(tot[c][:, ks(h)]) + upd[i, h, c]
            d.st[h] = st

    if emit:
        for i, (d, _, qe, _, _, _, _) in enumerate(work):
            for h in heads:
                for c in range(nch):
                    inter = lax.dot_general(qe[rows(c), ks(h)], seen[i, h, c], nt, preferred_element_type=F32)
                    d.o[rows(c), vs(h)] = (intra[i, h][rows(c)] + inter).astype(d.o.dtype)


def _gla_kernel(qf, kf, vf, lrf, cosf, sinf, qb, kb, vb, lrb, cosb, sinb,
                wdf, bdf, wdb, bdb, cmf, cmb, of_ref, ob_ref, stf, stb, *, n_ctx_blk, **kw):
    s = pl.program_id(2)
    dirs = (_Dir(qf, kf, vf, lrf, cosf, sinf, wdf, bdf, cmf, of_ref, stf, False),
            _Dir(qb, kb, vb, lrb, cosb, sinb, wdb, bdb, cmb, ob_ref, stb, True))

    @pl.when(s == 0)
    def _():
        stf[...] = jnp.zeros_like(stf)
        stb[...] = jnp.zeros_like(stb)

    @pl.when(s < n_ctx_blk)
    def _():
        _gla_block(dirs, False, **kw)

    @pl.when(s >= n_ctx_blk)
    def _():
        _gla_block(dirs, True, **kw)


def _gla(z, lr, cos_t, sin_t, wdf, bdf, wdb, bdb, cmf, cmb, *, bsz, s_len, c_len, tb, hb, dk, dv):
    n_ctx_blk, n_x_blk = c_len // tb, s_len // tb
    nb = n_ctx_blk + n_x_blk
    key_w = GLA_HEADS * dk
    qw, vw = hb * dk, hb * dv
    k_off, v_off = key_w // qw, (2 * key_w) // vw

    def fblk(s):
        return s

    def bblk(s):
        return jnp.where(s < n_ctx_blk, n_ctx_blk - 1 - s, nb - 1 - s + n_ctx_blk)

    def specs(blk):
        return [
            pl.BlockSpec((tb, qw), lambda b, g, s: (b * nb + blk(s), g)),
            pl.BlockSpec((tb, qw), lambda b, g, s: (b * nb + blk(s), k_off + g)),
            pl.BlockSpec((tb, vw), lambda b, g, s: (b * nb + blk(s), v_off + g)),
            pl.BlockSpec((tb, V7X_LANES), lambda b, g, s: (b * nb + blk(s), 0)),
            pl.BlockSpec((tb, qw), lambda b, g, s: (blk(s), 0)),
            pl.BlockSpec((tb, qw), lambda b, g, s: (blk(s), 0)),
        ]

    def wspec():
        return [pl.BlockSpec((V7X_LANES, qw), lambda b, g, s: (0, g)),
                pl.BlockSpec((1, qw), lambda b, g, s: (0, g))]

    def oblk_f(b, g, s):
        return (b * n_x_blk + jnp.maximum(s - n_ctx_blk, 0), g)

    def oblk_b(b, g, s):
        return (b * n_x_blk + jnp.minimum(nb - 1 - s + n_ctx_blk, nb - 1) - n_ctx_blk, g)

    cspec = pl.BlockSpec((tb, tb), lambda b, g, s: (0, 0))
    kern = functools.partial(_gla_kernel, n_ctx_blk=n_ctx_blk, hb=hb, dk=dk, dv=dv, tb=tb)
    o_shape = jax.ShapeDtypeStruct((bsz * s_len, GLA_HEADS * dv), BF16)
    return pl.pallas_call(
        kern,
        grid=(bsz, GLA_HEADS // hb, nb),
        in_specs=specs(fblk) + specs(bblk) + wspec() + wspec() + [cspec, cspec],
        out_specs=[pl.BlockSpec((tb, vw), oblk_f), pl.BlockSpec((tb, vw), oblk_b)],
        out_shape=[o_shape, o_shape],
        scratch_shapes=[pltpu.VMEM((hb, dv, dk), F32), pltpu.VMEM((hb, dv, dk), F32)],
        compiler_params=_params(("arbitrary", "arbitrary", "arbitrary")),
        name="gla_scan",
    )(z, z, z, lr, cos_t, sin_t, z, z, z, lr, cos_t, sin_t, wdf, bdf, wdb, bdb, cmf, cmb)


def _mix_kernel(of_ref, ob_ref, r_ref, u_ref, vv_ref, gn_ref, lg_ref, lb_ref, ws_ref, bs_ref, y_ref,
                *, dv, val_w, groups, tb):
    o = of_ref[...].astype(F32) + ob_ref[...].astype(F32)
    for h in range(val_w // dv):
        cs = slice(h * dv, (h + 1) * dv)
        r = r_ref[:, cs].astype(F32)
        y_ref[:, cs] = (_rms(o[:, cs], gn_ref[:, cs]) * _silu(r)).astype(y_ref.dtype)

    u = u_ref[...].astype(F32)
    vv = vv_ref[...].astype(F32)
    mu = jnp.mean(vv, axis=-1, keepdims=True)
    cen = vv - mu
    var = jnp.mean(cen * cen, axis=-1, keepdims=True)
    vn = (cen * lax.rsqrt(var + EPS) * lg_ref[...] + lb_ref[...]).astype(BF16)
    sgw = vn.shape[1]
    gw = sgw // groups
    for c in range(tb // SG_CHUNK):
        rows = slice(c * SG_CHUNK, (c + 1) * SG_CHUNK)
        for g in range(groups):
            cs = slice(g * gw, (g + 1) * gw)
            sg = jnp.dot(ws_ref[g], vn[rows, cs], preferred_element_type=F32) + bs_ref[:, cs]
            y_ref[rows, val_w + g * gw:val_w + (g + 1) * gw] = (u[rows, cs] * sg).astype(y_ref.dtype)


def _mix(o_f, o_b, z, zg, gn, lg, lb, ws, bsx, *, bsz, s_len, c_len, tb, dv, val_w, sgw, r_off):
    t = bsz * s_len
    n_x, nb = s_len // tb, (s_len + c_len) // tb
    groups = ws.shape[0]

    def zrow(i):
        return (i // n_x) * nb + c_len // tb + i % n_x

    kern = functools.partial(_mix_kernel, dv=dv, val_w=val_w, groups=groups, tb=tb)
    return pl.pallas_call(
        kern,
        grid=(t // tb,),
        in_specs=[pl.BlockSpec((tb, val_w), lambda i: (i, 0)),
                  pl.BlockSpec((tb, val_w), lambda i: (i, 0)),
                  pl.BlockSpec((tb, val_w), lambda i: (zrow(i), r_off // val_w)),
                  pl.BlockSpec((tb, sgw), lambda i: (zrow(i), 0)),
                  pl.BlockSpec((tb, sgw), lambda i: (zrow(i), 1)),
                  pl.BlockSpec((1, val_w), lambda i: (0, 0)),
                  pl.BlockSpec((1, sgw), lambda i: (0, 0)),
                  pl.BlockSpec((1, sgw), lambda i: (0, 0)),
                  pl.BlockSpec((groups, SG_CHUNK, SG_CHUNK), lambda i: (0, 0, 0)),
                  pl.BlockSpec((SG_CHUNK, sgw), lambda i: (0, 0))],
        out_specs=pl.BlockSpec((tb, val_w + sgw), lambda i: (i, 0)),
        out_shape=jax.ShapeDtypeStruct((t, val_w + sgw), BF16),
        compiler_params=_params(("arbitrary",)),
        name="mix_readout",
    )(o_f, o_b, z, zg, zg, gn, lg, lb, ws, bsx)


def _res1_kernel(x_ref, m_ref, g1_ref, sh_ref, sc_ref, pg_ref, ng_ref, x1_ref, h2_ref):
    x1 = x_ref[...] + g1_ref[...] * _rms(m_ref[...].astype(F32), pg_ref[...])
    x1_ref[...] = x1
    h2_ref[...] = (_rms(x1, ng_ref[...]) * (1.0 + sc_ref[...]) + sh_ref[...]).astype(h2_ref.dtype)


def _res1(x2, mix, mod3, post_g, pre_g, *, s_len, tb):
    t, d = x2.shape
    n_x = s_len // tb
    row = pl.BlockSpec((tb, d), lambda i: (i, 0))
    vec = pl.BlockSpec((1, d), lambda i: (0, 0))

    def modc(col):
        return pl.BlockSpec((None, 1, d), lambda i: (i // n_x, 0, col))

    return pl.pallas_call(
        _res1_kernel,
        grid=(t // tb,),
        in_specs=[row, row, modc(2), modc(3), modc(4), vec, vec],
        out_specs=[row, row],
        out_shape=[jax.ShapeDtypeStruct((t, d), F32), jax.ShapeDtypeStruct((t, d), BF16)],
        compiler_params=_params(("arbitrary",)),
        name="residual1",
    )(x2, mix, mod3, mod3, mod3, post_g, pre_g)


def _res2_kernel(x_ref, m_ref, g2_ref, pg_ref, o_ref):
    o_ref[...] = x_ref[...] + g2_ref[...] * _rms(m_ref[...].astype(F32), pg_ref[...])


def _res2(x1, m2, mod3, post_g, *, s_len, tb):
    t, d = x1.shape
    n_x = s_len // tb
    row = pl.BlockSpec((tb, d), lambda i: (i, 0))
    return pl.pallas_call(
        _res2_kernel,
        grid=(t // tb,),
        in_specs=[row, row, pl.BlockSpec((None, 1, d), lambda i: (i // n_x, 0, 5)),
                  pl.BlockSpec((1, d), lambda i: (0, 0))],
        out_specs=row,
        out_shape=jax.ShapeDtypeStruct((t, d), F32),
        compiler_params=_params(("arbitrary",)),
        name="residual2",
    )(x1, m2, mod3, post_g)


def _rope_tables(s_len, c_len, dk, hb):
    m = dk // 4
    inv_freq = ROPE_BASE ** (-jnp.arange(m, dtype=F32) / m)
    pos = jnp.arange(s_len)
    ang_r = (pos // GRID_W).astype(F32)[:, None] * inv_freq[None, :]
    ang_c = (pos % GRID_W).astype(F32)[:, None] * inv_freq[None, :]
    cos = jnp.concatenate([jnp.cos(ang_r)] * 2 + [jnp.cos(ang_c)] * 2, axis=-1)
    sin = jnp.concatenate([-jnp.sin(ang_r), jnp.sin(ang_r), -jnp.sin(ang_c), jnp.sin(ang_c)], axis=-1)
    cos = jnp.concatenate([jnp.ones((c_len, dk), F32), cos], axis=0)
    sin = jnp.concatenate([jnp.zeros((c_len, dk), F32), sin], axis=0)
    return jnp.tile(cos, (1, hb)), jnp.tile(sin, (1, hb))


def _chunk_matrices(tb):
    i = jnp.arange(tb)
    same = (i[:, None] // GLA_CHUNK) == (i[None, :] // GLA_CHUNK)
    return ((same & (i[None, :] <= i[:, None])).astype(BF16),
            (same & (i[None, :] >= i[:, None])).astype(BF16))


def _tile(n, pref):
    return pref if n % pref == 0 else n


def kernel(x, c, ctx, c_ctx, w_ada, b_ada, pre1_g, post1_g, pre2_g, post2_g, w_in, w_dec_f, b_dec_f,
           w_dec_b, b_dec_b, gla_norm_g, sg_ln_g, sg_ln_b, w_s, b_s, w_o, w_1, w_2):
    bsz, s_len, d = x.shape
    c_len = ctx.shape[1]
    assert w_ada.shape[0] == 1, "single-layer block only"
    assert bsz < MOD_ROWS
    lowrank, key_w = w_dec_f.shape[1], w_dec_f.shape[2]
    dk = key_w // GLA_HEADS
    dv = gla_norm_g.shape[2]
    val_w = GLA_HEADS * dv
    sgw = sg_ln_g.shape[1]
    lf0 = 2 * key_w + 2 * val_w
    sg0 = lf0 + 2 * lowrank
    assert dk == V7X_LANES and 2 * lowrank <= V7X_LANES and w_in.shape[2] == sg0 + 2 * sgw
    t = bsz * s_len

    w_in_t = jnp.swapaxes(w_in[0], 0, 1).astype(BF16)
    w_sg_t = w_in_t[sg0:]
    w_lr_t = jnp.pad(w_in_t[lf0:sg0], ((0, V7X_LANES - 2 * lowrank), (0, 0)))
    wdf = jnp.pad(w_dec_f[0], ((0, V7X_LANES - lowrank), (0, 0)))
    wdb = jnp.pad(w_dec_b[0], ((lowrank, V7X_LANES - 2 * lowrank), (0, 0)))

    c_all = jnp.concatenate([c, c_ctx[None], jnp.zeros((MOD_ROWS - bsz - 1, d), F32)], axis=0)
    mod = _ada(c_all, w_ada[0], b_ada[0])
    mod3 = mod.reshape(MOD_ROWS, 1, N_MOD * d)

    rb = _tile(math.gcd(s_len, c_len), 256)
    hx = _prenorm(x, ctx, mod3, pre1_g, rb)

    m_all = hx.shape[0]
    bm_all = _tile(m_all, 1024)
    z, lr, w_o_b = _inproj(hx, w_in_t, lf0, w_lr_t, w_o[0], bm_all, _tile(lf0, 1024))
    zg = _matmul(hx, w_sg_t, BF16, bm_all, _tile(2 * sgw, 1024), d, act="gelu", w_t=True, name="in_proj_sg")

    hb = min(GLA_HEADS, 4)
    cos_t, sin_t = _rope_tables(s_len, c_len, dk, hb)
    cmf, cmb = _chunk_matrices(rb)
    o_f, o_b = _gla(z, lr, cos_t, sin_t, wdf, b_dec_f, wdb, b_dec_b, cmf, cmb,
                    bsz=bsz, s_len=s_len, c_len=c_len, tb=rb, hb=hb, dk=dk, dv=dv)

    bsx = jnp.repeat(b_s[0].T, sgw // w_s.shape[1], axis=1)
    y = _mix(o_f, o_b, z, zg, gla_norm_g.reshape(1, val_w), sg_ln_g, sg_ln_b, w_s[0].astype(BF16), bsx,
             bsz=bsz, s_len=s_len, c_len=c_len, tb=rb, dv=dv, val_w=val_w, sgw=sgw, r_off=lf0 - val_w)

    bm = _tile(t, 1024)
    mix, w_1_b = _matmul(y, w_o_b, BF16, bm, _tile(d, 1024), d, cast_src=w_1[0], name="out_proj")
    x2 = x.reshape(t, d)
    x1, h2 = _res1(x2, mix, mod3, post1_g, pre2_g, s_len=s_len, tb=rb)
    d_ff = w_1_b.shape[1]
    hmid, w_2_b = _matmul(h2, w_1_b, BF16, bm, _tile(d_ff, 1024), d, act="relu2", cast_src=w_2[0],
                          name="mlp_up")
    m2 = _matmul(hmid, w_2_b, BF16, bm, _tile(d, 1024), _tile(d_ff, 4096), name="mlp_down")
    out = _res2(x1, m2, mod3, post2_g, s_len=s_len, tb=rb)
    return out.reshape(bsz, s_len, d)
```

```python
import collections
import functools
import math

import jax
import jax.numpy as jnp
from jax import lax
from jax.experimental import pallas as pl
from jax.experimental.pallas import tpu as pltpu

GRID_W = 64
GLA_HEADS = 8
GLA_TAU = 16.0
ROPE_BASE = 10000.0
N_MOD = 6
EPS = 1e-6
SG_CHUNK = 128
GLA_CHUNK = 64
LOG2_E = math.log2(math.e)

V7X_LANES = 128
MOD_ROWS = 8
VMEM_LIMIT_BYTES = 56 * 1024 * 1024

F32 = jnp.float32
BF16 = jnp.bfloat16


def _params(sem, vmem=VMEM_LIMIT_BYTES):
    return pltpu.CompilerParams(dimension_semantics=sem, vmem_limit_bytes=vmem)


def _rms(t, g):
    return t * lax.rsqrt(jnp.mean(t * t, axis=-1, keepdims=True) + EPS) * g


def _silu(t):
    return t * (1.0 / (1.0 + jnp.exp(-t)))


def _gelu(t):
    return 0.5 * t * (1.0 + lax.erf(t * (2.0 ** -0.5)))


def _ada_kernel(c_ref, w_ref, b_ref, o_ref):
    cond = _silu(c_ref[...])
    o_ref[...] = jnp.dot(cond.astype(BF16), w_ref[...].astype(BF16),
                         preferred_element_type=F32) + b_ref[...]


def _ada(c_all, w_ada, b_ada):
    d, n = w_ada.shape
    bn = min(n, 1024)
    return pl.pallas_call(
        _ada_kernel,
        grid=(n // bn,),
        in_specs=[pl.BlockSpec((MOD_ROWS, d), lambda j: (0, 0)),
                  pl.BlockSpec((d, bn), lambda j: (0, j)),
                  pl.BlockSpec((1, bn), lambda j: (0, j))],
        out_specs=pl.BlockSpec((MOD_ROWS, bn), lambda j: (0, j)),
        out_shape=jax.ShapeDtypeStruct((MOD_ROWS, n), F32),
        compiler_params=_params(("arbitrary",)),
        name="ada_mod",
    )(c_all, w_ada, b_ada.reshape(1, n))


def _prenorm_kernel(x_ref, ctx_ref, mod_ref, g_ref, o_ref, *, n_ctx_blk, d):
    j = pl.program_id(1)

    def emit(t):
        y = _rms(t, g_ref[...])
        o_ref[...] = (y * (1.0 + mod_ref[:, d:2 * d]) + mod_ref[:, 0:d]).astype(o_ref.dtype)

    @pl.when(j < n_ctx_blk)
    def _():
        emit(ctx_ref[...])

    @pl.when(j >= n_ctx_blk)
    def _():
        emit(x_ref[...])


def _prenorm(x, ctx, mod3, g, rb):
    bsz, s, d = x.shape
    c = ctx.shape[1]
    n_ctx_blk, nb = c // rb, (c + s) // rb
    return pl.pallas_call(
        functools.partial(_prenorm_kernel, n_ctx_blk=n_ctx_blk, d=d),
        grid=(bsz, nb),
        in_specs=[
            pl.BlockSpec((None, rb, d), lambda b, j: (b, jnp.maximum(j - n_ctx_blk, 0), 0)),
            pl.BlockSpec((None, rb, d), lambda b, j: (b, jnp.minimum(j, n_ctx_blk - 1), 0)),
            pl.BlockSpec((None, 1, 2 * d), lambda b, j: (jnp.where(j < n_ctx_blk, bsz, b), 0, 0)),
            pl.BlockSpec((1, d), lambda b, j: (0, 0)),
        ],
        out_specs=pl.BlockSpec((rb, d), lambda b, j: (b * nb + j, 0)),
        out_shape=jax.ShapeDtypeStruct((bsz * (c + s), d), BF16),
        compiler_params=_params(("arbitrary", "arbitrary")),
        name="prenorm_mod",
    )(x, ctx, mod3, g)


CAST_ROWS = 64


def _cast_specs(src, nsteps, lin):
    rows, cols = src.shape
    cr = CAST_ROWS
    while rows % cr or rows // cr > nsteps:
        cr += CAST_ROWS
    nblk = rows // cr
    spec = pl.BlockSpec((cr, cols), lambda *g: (jnp.minimum(lin(*g), nblk - 1), 0))
    return spec, spec, jax.ShapeDtypeStruct((rows, cols), BF16)


_NT = (((1,), (1,)), ((), ()))


def _qkproj_kernel(a_ref, w_ref, cos_ref, sin_ref, z_ref, *, dk):
    r = lax.dot_general(a_ref[...], w_ref[...], _NT, preferred_element_type=F32)
    cos, sin = cos_ref[...], sin_ref[...]
    for h in range(r.shape[1] // dk):
        cs = slice(h * dk, (h + 1) * dk)
        z_ref[:, cs] = (r[:, cs] * cos + pltpu.roll(r[:, cs], dk // 2, 1) * sin).astype(z_ref.dtype)


def _qkproj(a, w_t, cos_t, sin_t, bm, bn, *, dk):
    m, k = a.shape
    n = w_t.shape[0]
    return pl.pallas_call(
        functools.partial(_qkproj_kernel, dk=dk),
        grid=(m // bm, n // bn),
        in_specs=[pl.BlockSpec((bm, k), lambda i, j: (i, 0)),
                  pl.BlockSpec((bn, k), lambda i, j: (j, 0)),
                  pl.BlockSpec((bm, dk), lambda i, j: (i, 0)),
                  pl.BlockSpec((bm, dk), lambda i, j: (i, 0))],
        out_specs=pl.BlockSpec((bm, bn), lambda i, j: (i, j)),
        out_shape=jax.ShapeDtypeStruct((m, n), BF16),
        compiler_params=_params(("arbitrary", "arbitrary")),
        name="in_proj_qk",
    )(a, w_t, cos_t, sin_t)


def _inproj_kernel(a_ref, w_ref, wlr_ref, src_ref, z_ref, lr_ref, dst_ref):
    a = a_ref[...]
    z_ref[...] = lax.dot_general(a, w_ref[...], _NT, preferred_element_type=F32).astype(z_ref.dtype)
    dst_ref[...] = src_ref[...].astype(dst_ref.dtype)

    @pl.when(pl.program_id(1) == 0)
    def _():
        lr_ref[...] = lax.dot_general(a, wlr_ref[...], _NT, preferred_element_type=F32)


def _inproj(a, w_t, row0, n, wlr_t, cast_src, bm, bn):
    m, k = a.shape
    ni, nj, j0 = m // bm, n // bn, row0 // bn
    assert row0 % bn == 0
    c_in, c_out, c_shape = _cast_specs(cast_src, ni * nj, lambda i, j: i * nj + j)
    return pl.pallas_call(
        _inproj_kernel,
        grid=(ni, nj),
        in_specs=[pl.BlockSpec((bm, k), lambda i, j: (i, 0)),
                  pl.BlockSpec((bn, k), lambda i, j: (j0 + j, 0)),
                  pl.BlockSpec((V7X_LANES, k), lambda i, j: (0, 0)),
                  c_in],
        out_specs=[pl.BlockSpec((bm, bn), lambda i, j: (i, j)),
                   pl.BlockSpec((bm, V7X_LANES), lambda i, j: (i, 0)),
                   c_out],
        out_shape=[jax.ShapeDtypeStruct((m, n), BF16),
                   jax.ShapeDtypeStruct((m, V7X_LANES), F32),
                   c_shape],
        compiler_params=_params(("arbitrary", "arbitrary")),
        name="in_proj_vr",
    )(a, w_t, wlr_t, cast_src)


def _matmul_kernel(a_ref, w_ref, *rest, nk, act, cast, w_t):
    if cast:
        src_ref, o_ref, dst_ref = rest[:3]
        dst_ref[...] = src_ref[...].astype(dst_ref.dtype)
    else:
        o_ref = rest[0]

    def product():
        if w_t:
            return lax.dot_general(a_ref[...], w_ref[...], _NT, preferred_element_type=F32)
        return jnp.dot(a_ref[...], w_ref[...], preferred_element_type=F32)

    def finish(r):
        if act == "relu2":
            r = jnp.square(jnp.maximum(r, 0.0))
        elif act == "gelu":
            r = _gelu(r)
        o_ref[...] = r.astype(o_ref.dtype)

    if nk == 1:
        finish(product())
        return
    acc_ref = rest[-1]
    kk = pl.program_id(2)

    @pl.when(kk == 0)
    def _():
        acc_ref[...] = product()

    @pl.when(jnp.logical_and(kk > 0, kk < nk - 1))
    def _():
        acc_ref[...] += product()

    @pl.when(kk == nk - 1)
    def _():
        finish(acc_ref[...] + product())


def _matmul(a, w, out_dtype, bm, bn, bk, act=None, cast_src=None, w_t=False, name="matmul"):
    m, k = a.shape
    n = w.shape[0] if w_t else w.shape[1]
    ni, nj, nk = m // bm, n // bn, k // bk
    w_spec = (pl.BlockSpec((bn, bk), lambda i, j, kk: (j, kk)) if w_t
              else pl.BlockSpec((bk, bn), lambda i, j, kk: (kk, j)))
    in_specs = [pl.BlockSpec((bm, bk), lambda i, j, kk: (i, kk)), w_spec]
    out_specs = [pl.BlockSpec((bm, bn), lambda i, j, kk: (i, j))]
    out_shape = [jax.ShapeDtypeStruct((m, n), out_dtype)]
    args = [a, w]
    if cast_src is not None:
        c_in, c_out, c_shape = _cast_specs(cast_src, ni * nj * nk, lambda i, j, kk: (i * nj + j) * nk + kk)
        in_specs.append(c_in)
        out_specs.append(c_out)
        out_shape.append(c_shape)
        args.append(cast_src)
    res = pl.pallas_call(
        functools.partial(_matmul_kernel, nk=nk, act=act, cast=cast_src is not None, w_t=w_t),
        grid=(ni, nj, nk),
        in_specs=in_specs,
        out_specs=out_specs,
        out_shape=out_shape,
        scratch_shapes=[pltpu.VMEM((bm, bn), F32)] if nk > 1 else [],
        compiler_params=_params(("arbitrary", "arbitrary", "arbitrary")),
        name=name,
    )(*args)
    return res if cast_src is not None else res[0]


_Dir = collections.namedtuple("_Dir", "q k v lr wd bd cmat o st reverse")


def _gla_block(dirs, emit, *, hb, dk, dv, tb):
    nch = tb // GLA_CHUNK
    nt = _NT
    tn = (((0,), (0,)), ((), ()))

    def ks(h):
        return slice(h * dk, (h + 1) * dk)

    def vs(h):
        return slice(h * dv, (h + 1) * dv)

    def rows(c):
        return slice(c * GLA_CHUNK, (c + 1) * GLA_CHUNK)

    gates = []
    for d in dirs:
        a = jnp.dot(d.lr[...].astype(BF16), d.wd[...].astype(BF16), preferred_element_type=F32) + d.bd[...]
        la = (jnp.minimum(a, 0.0) - jnp.log(1.0 + jnp.exp(-jnp.abs(a)))) * (LOG2_E / GLA_TAU)
        hi = la.astype(BF16)
        lo = (la - hi.astype(F32)).astype(BF16)
        tri = d.cmat[...]
        gates.append((jnp.dot(tri, hi, preferred_element_type=F32)
                      + jnp.dot(tri, lo, preferred_element_type=F32), tri))

    work = []
    for d, (cum, tri) in zip(dirs, gates):
        kr = d.k[...].astype(F32)
        ends = [c * GLA_CHUNK if d.reverse else (c + 1) * GLA_CHUNK - 1 for c in range(nch)]
        tot = [cum[e:e + 1] for e in ends]
        kd = [(kr[rows(c)] * jnp.exp2(tot[c] - cum[rows(c)])).astype(BF16) for c in range(nch)]
        qe = ke = None
        if emit:
            qe = (d.q[...].astype(F32) * (jnp.exp2(cum) * dk ** -0.5)).astype(BF16)
            ke = (kr * jnp.exp2(-cum)).astype(BF16)
        work.append((d, d.v[...], qe, ke, kd, tot, tri))

    heads = range(hb)
    intra = {}
    if emit:
        att = {(i, h): lax.dot_general(qe[:, ks(h)], ke[:, ks(h)], nt, preferred_element_type=F32)
               for i, (_, _, qe, ke, _, _, _) in enumerate(work) for h in heads}
        for i, (_, v, _, _, _, _, tri) in enumerate(work):
            for h in heads:
                att_m = jnp.where(tri > 0, att[i, h].astype(BF16), jnp.zeros((), BF16))
                intra[i, h] = jnp.dot(att_m, v[:, vs(h)], preferred_element_type=F32)

    upd = {(i, h, c): lax.dot_general(v[rows(c), vs(h)], kd[c][:, ks(h)], tn, preferred_element_type=F32)
           for i, (_, v, _, _, kd, _, _) in enumerate(work) for h in heads for c in range(nch)}

    seen = {}
    for i, (d, _, _, _, _, tot, _) in enumerate(work):
        order = range(nch - 1, -1, -1) if d.reverse else range(nch)
        for h in heads:
            st = d.st[h]
            for c in order:
                if emit:
                    seen[i, h, c] = st.astype(BF16)
                st = st * jnp.exp2(tot[c][:, ks(h)]) + upd[i, h, c]
            d.st[h] = st

    if emit:
        for i, (d, _, qe, _, _, _, _) in enumerate(work):
            for h in heads:
                for c in range(nch):
                    inter = lax.dot_general(qe[rows(c), ks(h)], seen[i, h, c], nt, preferred_element_type=F32)
                    d.o[rows(c), vs(h)] = (intra[i, h][rows(c)] + inter).astype(d.o.dtype)


def _gla_kernel(qf, kf, vf, lrf, qb, kb, vb, lrb,
                wdf, bdf, wdb, bdb, cmf, cmb, of_ref, ob_ref, stf, stb, *, n_ctx_blk, **kw):
    s = pl.program_id(2)
    dirs = (_Dir(qf, kf, vf, lrf, wdf, bdf, cmf, of_ref, stf, False),
            _Dir(qb, kb, vb, lrb, wdb, bdb, cmb, ob_ref, stb, True))

    @pl.when(s == 0)
    def _():
        stf[...] = jnp.zeros_like(stf)
        stb[...] = jnp.zeros_like(stb)

    @pl.when(s < n_ctx_blk)
    def _():
        _gla_block(dirs, False, **kw)

    @pl.when(s >= n_ctx_blk)
    def _():
        _gla_block(dirs, True, **kw)


def _gla(zqk, zvr, lr, wdf, bdf, wdb, bdb, cmf, cmb, *, bsz, s_len, c_len, tb, hb, dk, dv):
    n_ctx_blk, n_x_blk = c_len // tb, s_len // tb
    nb = n_ctx_blk + n_x_blk
    key_w = GLA_HEADS * dk
    qw, vw = hb * dk, hb * dv
    k_off = key_w // qw

    def fblk(s):
        return s

    def bblk(s):
        return jnp.where(s < n_ctx_blk, n_ctx_blk - 1 - s, nb - 1 - s + n_ctx_blk)

    def specs(blk):
        return [
            pl.BlockSpec((tb, qw), lambda b, g, s: (b * nb + blk(s), g)),
            pl.BlockSpec((tb, qw), lambda b, g, s: (b * nb + blk(s), k_off + g)),
            pl.BlockSpec((tb, vw), lambda b, g, s: (b * nb + blk(s), g)),
            pl.BlockSpec((tb, V7X_LANES), lambda b, g, s: (b * nb + blk(s), 0)),
        ]

    def wspec():
        return [pl.BlockSpec((V7X_LANES, qw), lambda b, g, s: (0, g)),
                pl.BlockSpec((1, qw), lambda b, g, s: (0, g))]

    def oblk_f(b, g, s):
        return (b * n_x_blk + jnp.maximum(s - n_ctx_blk, 0), g)

    def oblk_b(b, g, s):
        return (b * n_x_blk + jnp.minimum(nb - 1 - s + n_ctx_blk, nb - 1) - n_ctx_blk, g)

    cspec = pl.BlockSpec((tb, tb), lambda b, g, s: (0, 0))
    kern = functools.partial(_gla_kernel, n_ctx_blk=n_ctx_blk, hb=hb, dk=dk, dv=dv, tb=tb)
    o_shape = jax.ShapeDtypeStruct((bsz * s_len, GLA_HEADS * dv), BF16)
    return pl.pallas_call(
        kern,
        grid=(bsz, GLA_HEADS // hb, nb),
        in_specs=specs(fblk) + specs(bblk) + wspec() + wspec() + [cspec, cspec],
        out_specs=[pl.BlockSpec((tb, vw), oblk_f), pl.BlockSpec((tb, vw), oblk_b)],
        out_shape=[o_shape, o_shape],
        scratch_shapes=[pltpu.VMEM((hb, dv, dk), F32), pltpu.VMEM((hb, dv, dk), F32)],
        compiler_params=_params(("arbitrary", "arbitrary", "arbitrary")),
        name="gla_scan",
    )(zqk, zqk, zvr, lr, zqk, zqk, zvr, lr, wdf, bdf, wdb, bdb, cmf, cmb)


def _mix_kernel(of_ref, ob_ref, r_ref, u_ref, vv_ref, gn_ref, lg_ref, lb_ref, ws_ref, bs_ref, y_ref,
                *, dv, val_w, groups, tb):
    o = of_ref[...].astype(F32) + ob_ref[...].astype(F32)
    for h in range(val_w // dv):
        cs = slice(h * dv, (h + 1) * dv)
        r = r_ref[:, cs].astype(F32)
        y_ref[:, cs] = (_rms(o[:, cs], gn_ref[:, cs]) * _silu(r)).astype(y_ref.dtype)

    u = u_ref[...].astype(F32)
    vv = vv_ref[...].astype(F32)
    mu = jnp.mean(vv, axis=-1, keepdims=True)
    cen = vv - mu
    var = jnp.mean(cen * cen, axis=-1, keepdims=True)
    vn = (cen * lax.rsqrt(var + EPS) * lg_ref[...] + lb_ref[...]).astype(BF16)
    sgw = vn.shape[1]
    gw = sgw // groups
    for c in range(tb // SG_CHUNK):
        rows = slice(c * SG_CHUNK, (c + 1) * SG_CHUNK)
        for g in range(groups):
            cs = slice(g * gw, (g + 1) * gw)
            sg = jnp.dot(ws_ref[g], vn[rows, cs], preferred_element_type=F32) + bs_ref[:, cs]
            y_ref[rows, val_w + g * gw:val_w + (g + 1) * gw] = (u[rows, cs] * sg).astype(y_ref.dtype)


def _mix(o_f, o_b, z, zg, gn, lg, lb, ws, bsx, *, bsz, s_len, c_len, tb, dv, val_w, sgw, r_off):
    t = bsz * s_len
    n_x, nb = s_len // tb, (s_len + c_len) // tb
    groups = ws.shape[0]

    def zrow(i):
        return (i // n_x) * nb + c_len // tb + i % n_x

    kern = functools.partial(_mix_kernel, dv=dv, val_w=val_w, groups=groups, tb=tb)
    return pl.pallas_call(
        kern,
        grid=(t // tb,),
        in_specs=[pl.BlockSpec((tb, val_w), lambda i: (i, 0)),
                  pl.BlockSpec((tb, val_w), lambda i: (i, 0)),
                  pl.BlockSpec((tb, val_w), lambda i: (zrow(i), r_off // val_w)),
                  pl.BlockSpec((tb, sgw), lambda i: (zrow(i), 0)),
                  pl.BlockSpec((tb, sgw), lambda i: (zrow(i), 1)),
                  pl.BlockSpec((1, val_w), lambda i: (0, 0)),
                  pl.BlockSpec((1, sgw), lambda i: (0, 0)),
                  pl.BlockSpec((1, sgw), lambda i: (0, 0)),
                  pl.BlockSpec((groups, SG_CHUNK, SG_CHUNK), lambda i: (0, 0, 0)),
                  pl.BlockSpec((SG_CHUNK, sgw), lambda i: (0, 0))],
        out_specs=pl.BlockSpec((tb, val_w + sgw), lambda i: (i, 0)),
        out_shape=jax.ShapeDtypeStruct((t, val_w + sgw), BF16),
        compiler_params=_params(("arbitrary",)),
        name="mix_readout",
    )(o_f, o_b, z, zg, zg, gn, lg, lb, ws, bsx)


def _res1_kernel(x_ref, m_ref, g1_ref, sh_ref, sc_ref, pg_ref, ng_ref, x1_ref, h2_ref):
    x1 = x_ref[...] + g1_ref[...] * _rms(m_ref[...].astype(F32), pg_ref[...])
    x1_ref[...] = x1
    h2_ref[...] = (_rms(x1, ng_ref[...]) * (1.0 + sc_ref[...]) + sh_ref[...]).astype(h2_ref.dtype)


def _res1(x2, mix, mod3, post_g, pre_g, *, s_len, tb):
    t, d = x2.shape
    n_x = s_len // tb
    row = pl.BlockSpec((tb, d), lambda i: (i, 0))
    vec = pl.BlockSpec((1, d), lambda i: (0, 0))

    def modc(col):
        return pl.BlockSpec((None, 1, d), lambda i: (i // n_x, 0, col))

    return pl.pallas_call(
        _res1_kernel,
        grid=(t // tb,),
        in_specs=[row, row, modc(2), modc(3), modc(4), vec, vec],
        out_specs=[row, row],
        out_shape=[jax.ShapeDtypeStruct((t, d), F32), jax.ShapeDtypeStruct((t, d), BF16)],
        compiler_params=_params(("arbitrary",)),
        name="residual1",
    )(x2, mix, mod3, mod3, mod3, post_g, pre_g)


def _res2_kernel(x_ref, m_ref, g2_ref, pg_ref, o_ref):
    o_ref[...] = x_ref[...] + g2_ref[...] * _rms(m_ref[...].astype(F32), pg_ref[...])


def _res2(x1, m2, mod3, post_g, *, s_len, tb):
    t, d = x1.shape
    n_x = s_len // tb
    row = pl.BlockSpec((tb, d), lambda i: (i, 0))
    return pl.pallas_call(
        _res2_kernel,
        grid=(t // tb,),
        in_specs=[row, row, pl.BlockSpec((None, 1, d), lambda i: (i // n_x, 0, 5)),
                  pl.BlockSpec((1, d), lambda i: (0, 0))],
        out_specs=row,
        out_shape=jax.ShapeDtypeStruct((t, d), F32),
        compiler_params=_params(("arbitrary",)),
        name="residual2",
    )(x1, m2, mod3, post_g)


def _pair_order(t, axis, dk):
    shp = t.shape
    t = t.reshape(shp[:axis] + (shp[axis] // dk, 2, 2, dk // 4) + shp[axis + 1:])
    return jnp.swapaxes(t, axis + 1, axis + 2).reshape(shp)


def _rope_tables(s_len, c_len, dk, bsz):
    m = dk // 4
    inv_freq = ROPE_BASE ** (-jnp.arange(m, dtype=F32) / m)
    pos = jnp.arange(s_len)
    ang_r = (pos // GRID_W).astype(F32)[:, None] * inv_freq[None, :]
    ang_c = (pos % GRID_W).astype(F32)[:, None] * inv_freq[None, :]
    cos = jnp.concatenate([jnp.cos(ang_r), jnp.cos(ang_c)] * 2, axis=-1)
    sin = jnp.concatenate([-jnp.sin(ang_r), -jnp.sin(ang_c), jnp.sin(ang_r), jnp.sin(ang_c)], axis=-1)
    cos = jnp.concatenate([jnp.ones((c_len, dk), F32), cos], axis=0)
    sin = jnp.concatenate([jnp.zeros((c_len, dk), F32), sin], axis=0)
    return jnp.tile(cos, (bsz, 1)), jnp.tile(sin, (bsz, 1))


def _chunk_matrices(tb):
    i = jnp.arange(tb)
    same = (i[:, None] // GLA_CHUNK) == (i[None, :] // GLA_CHUNK)
    return ((same & (i[None, :] <= i[:, None])).astype(BF16),
            (same & (i[None, :] >= i[:, None])).astype(BF16))


def _tile(n, pref):
    return pref if n % pref == 0 else n


def kernel(x, c, ctx, c_ctx, w_ada, b_ada, pre1_g, post1_g, pre2_g, post2_g, w_in, w_dec_f, b_dec_f,
           w_dec_b, b_dec_b, gla_norm_g, sg_ln_g, sg_ln_b, w_s, b_s, w_o, w_1, w_2):
    bsz, s_len, d = x.shape
    c_len = ctx.shape[1]
    assert w_ada.shape[0] == 1, "single-layer block only"
    assert bsz < MOD_ROWS
    lowrank, key_w = w_dec_f.shape[1], w_dec_f.shape[2]
    dk = key_w // GLA_HEADS
    dv = gla_norm_g.shape[2]
    val_w = GLA_HEADS * dv
    sgw = sg_ln_g.shape[1]
    lf0 = 2 * key_w + 2 * val_w
    sg0 = lf0 + 2 * lowrank
    assert dk == V7X_LANES and 2 * lowrank <= V7X_LANES and w_in.shape[2] == sg0 + 2 * sgw
    t = bsz * s_len

    w_in_t = jnp.swapaxes(w_in[0], 0, 1).astype(BF16)
    w_qk_t = _pair_order(w_in_t[:2 * key_w], 0, dk)
    w_sg_t = w_in_t[sg0:]
    w_lr_t = jnp.pad(w_in_t[lf0:sg0], ((0, V7X_LANES - 2 * lowrank), (0, 0)))
    wdf = jnp.pad(_pair_order(w_dec_f[0], 1, dk), ((0, V7X_LANES - lowrank), (0, 0)))
    wdb = jnp.pad(_pair_order(w_dec_b[0], 1, dk), ((lowrank, V7X_LANES - 2 * lowrank), (0, 0)))
    bdf, bdb = _pair_order(b_dec_f, 1, dk), _pair_order(b_dec_b, 1, dk)

    c_all = jnp.concatenate([c, c_ctx[None], jnp.zeros((MOD_ROWS - bsz - 1, d), F32)], axis=0)
    mod = _ada(c_all, w_ada[0], b_ada[0])
    mod3 = mod.reshape(MOD_ROWS, 1, N_MOD * d)

    rb = _tile(math.gcd(s_len, c_len), 256)
    hx = _prenorm(x, ctx, mod3, pre1_g, rb)

    m_all = hx.shape[0]
    bm_all = _tile(m_all, 1024)
    cos_t, sin_t = _rope_tables(s_len, c_len, dk, bsz)
    zqk = _qkproj(hx, w_qk_t, cos_t, sin_t, bm_all, _tile(2 * key_w, 1024), dk=dk)
    zvr, lr, w_o_b = _inproj(hx, w_in_t, 2 * key_w, 2 * val_w, w_lr_t, w_o[0], bm_all,
                             _tile(math.gcd(2 * key_w, 2 * val_w), 1024))
    zg = _matmul(hx, w_sg_t, BF16, bm_all, _tile(2 * sgw, 1024), d, act="gelu", w_t=True, name="in_proj_sg")

    cmf, cmb = _chunk_matrices(rb)
    o_f, o_b = _gla(zqk, zvr, lr, wdf, bdf, wdb, bdb, cmf, cmb,
                    bsz=bsz, s_len=s_len, c_len=c_len, tb=rb, hb=GLA_HEADS, dk=dk, dv=dv)

    bsx = jnp.repeat(b_s[0].T, sgw // w_s.shape[1], axis=1)
    y = _mix(o_f, o_b, zvr, zg, gla_norm_g.reshape(1, val_w), sg_ln_g, sg_ln_b, w_s[0].astype(BF16), bsx,
             bsz=bsz, s_len=s_len, c_len=c_len, tb=rb, dv=dv, val_w=val_w, sgw=sgw, r_off=val_w)

    bm = _tile(t, 1024)
    mix, w_1_b = _matmul(y, w_o_b, BF16, bm, _tile(d, 1024), d, cast_src=w_1[0], name="out_proj")
    x2 = x.reshape(t, d)
    x1, h2 = _res1(x2, mix, mod3, post1_g, pre2_g, s_len=s_len, tb=rb)
    d_ff = w_1_b.shape[1]
    hmid, w_2_b = _matmul(h2, w_1_b, BF16, bm, _tile(d_ff, 1024), d, act="relu2", cast_src=w_2[0],
                          name="mlp_up")
    m2 = _matmul(hmid, w_2_b, BF16, bm, _tile(d, 1024), _tile(d_ff, 4096), name="mlp_down")
    out = _res2(x1, m2, mod3, post2_g, s_len=s_len, tb=rb)
    return out.reshape(bsz, s_len, d)
```

```python
import collections
import functools
import math

import jax
import jax.numpy as jnp
from jax import lax
from jax.experimental import pallas as pl
from jax.experimental.pallas import tpu as pltpu

GRID_W = 64
GLA_HEADS = 8
GLA_TAU = 16.0
ROPE_BASE = 10000.0
N_MOD = 6
EPS = 1e-6
SG_CHUNK = 128
GLA_CHUNK = 64
LOG2_E = math.log2(math.e)

V7X_LANES = 128
MOD_ROWS = 8
VMEM_LIMIT_BYTES = 56 * 1024 * 1024

F32 = jnp.float32
BF16 = jnp.bfloat16


def _params(sem, vmem=VMEM_LIMIT_BYTES):
    return pltpu.CompilerParams(dimension_semantics=sem, vmem_limit_bytes=vmem)


def _rms(t, g):
    return t * lax.rsqrt(jnp.mean(t * t, axis=-1, keepdims=True) + EPS) * g


def _silu(t):
    return t * (1.0 / (1.0 + jnp.exp(-t)))


def _gelu(t):
    return 0.5 * t * (1.0 + lax.erf(t * (2.0 ** -0.5)))


def _ada_kernel(c_ref, w_ref, b_ref, o_ref):
    cond = _silu(c_ref[...])
    o_ref[...] = jnp.dot(cond.astype(BF16), w_ref[...].astype(BF16),
                         preferred_element_type=F32) + b_ref[...]


def _ada(c_all, w_ada, b_ada):
    d, n = w_ada.shape
    bn = min(n, 1024)
    return pl.pallas_call(
        _ada_kernel,
        grid=(n // bn,),
        in_specs=[pl.BlockSpec((MOD_ROWS, d), lambda j: (0, 0)),
                  pl.BlockSpec((d, bn), lambda j: (0, j)),
                  pl.BlockSpec((1, bn), lambda j: (0, j))],
        out_specs=pl.BlockSpec((MOD_ROWS, bn), lambda j: (0, j)),
        out_shape=jax.ShapeDtypeStruct((MOD_ROWS, n), F32),
        compiler_params=_params(("arbitrary",)),
        name="ada_mod",
    )(c_all, w_ada, b_ada.reshape(1, n))


def _prenorm_kernel(x_ref, ctx_ref, mod_ref, g_ref, o_ref, *, n_ctx_blk, d):
    j = pl.program_id(1)

    def emit(t):
        y = _rms(t, g_ref[...])
        o_ref[...] = (y * (1.0 + mod_ref[:, d:2 * d]) + mod_ref[:, 0:d]).astype(o_ref.dtype)

    @pl.when(j < n_ctx_blk)
    def _():
        emit(ctx_ref[...])

    @pl.when(j >= n_ctx_blk)
    def _():
        emit(x_ref[...])


def _row_block(b, blk, bsz, n_ctx_blk, n_x_blk):
    return jnp.where(blk < n_ctx_blk, b * n_ctx_blk + blk, bsz * n_ctx_blk + b * n_x_blk + blk - n_ctx_blk)


def _prenorm(x, ctx, mod3, g, rb):
    bsz, s, d = x.shape
    c = ctx.shape[1]
    n_ctx_blk, nb = c // rb, (c + s) // rb
    return pl.pallas_call(
        functools.partial(_prenorm_kernel, n_ctx_blk=n_ctx_blk, d=d),
        grid=(bsz, nb),
        in_specs=[
            pl.BlockSpec((None, rb, d), lambda b, j: (b, jnp.maximum(j - n_ctx_blk, 0), 0)),
            pl.BlockSpec((None, rb, d), lambda b, j: (b, jnp.minimum(j, n_ctx_blk - 1), 0)),
            pl.BlockSpec((None, 1, 2 * d), lambda b, j: (jnp.where(j < n_ctx_blk, bsz, b), 0, 0)),
            pl.BlockSpec((1, d), lambda b, j: (0, 0)),
        ],
        out_specs=pl.BlockSpec((rb, d), lambda b, j: (_row_block(b, j, bsz, n_ctx_blk, nb - n_ctx_blk), 0)),
        out_shape=jax.ShapeDtypeStruct((bsz * (c + s), d), BF16),
        compiler_params=_params(("arbitrary", "arbitrary")),
        name="prenorm_mod",
    )(x, ctx, mod3, g)


CAST_ROWS = 64


def _cast_specs(src, nsteps, lin):
    rows, cols = src.shape
    cr = CAST_ROWS
    while rows % cr or rows // cr > nsteps:
        cr += CAST_ROWS
    nblk = rows // cr
    spec = pl.BlockSpec((cr, cols), lambda *g: (jnp.minimum(lin(*g), nblk - 1), 0))
    return spec, spec, jax.ShapeDtypeStruct((rows, cols), BF16)


_NT = (((1,), (1,)), ((), ()))


def _qkproj_kernel(a_ref, w_ref, cos_ref, sin_ref, z_ref, *, dk):
    r = lax.dot_general(a_ref[...], w_ref[...], _NT, preferred_element_type=F32)
    cos, sin = cos_ref[...], sin_ref[...]
    for h in range(r.shape[1] // dk):
        cs = slice(h * dk, (h + 1) * dk)
        z_ref[:, cs] = (r[:, cs] * cos + pltpu.roll(r[:, cs], dk // 2, 1) * sin).astype(z_ref.dtype)


def _qkproj(a, w_t, cos_t, sin_t, bm, bn, *, dk):
    m, k = a.shape
    n = w_t.shape[0]
    return pl.pallas_call(
        functools.partial(_qkproj_kernel, dk=dk),
        grid=(m // bm, n // bn),
        in_specs=[pl.BlockSpec((bm, k), lambda i, j: (i, 0)),
                  pl.BlockSpec((bn, k), lambda i, j: (j, 0)),
                  pl.BlockSpec((bm, dk), lambda i, j: (i, 0)),
                  pl.BlockSpec((bm, dk), lambda i, j: (i, 0))],
        out_specs=pl.BlockSpec((bm, bn), lambda i, j: (i, j)),
        out_shape=jax.ShapeDtypeStruct((m, n), BF16),
        compiler_params=_params(("arbitrary", "arbitrary")),
        name="in_proj_qk",
    )(a, w_t, cos_t, sin_t)


def _inproj_kernel(a_ref, w_ref, wlr_ref, src_ref, z_ref, lr_ref, dst_ref):
    a = a_ref[...]
    z_ref[...] = lax.dot_general(a, w_ref[...], _NT, preferred_element_type=F32).astype(z_ref.dtype)
    dst_ref[...] = src_ref[...].astype(dst_ref.dtype)

    @pl.when(pl.program_id(1) == 0)
    def _():
        lr_ref[...] = lax.dot_general(a, wlr_ref[...], _NT, preferred_element_type=F32)


def _inproj(a, w_t, row0, n, wlr_t, cast_src, bm, bn):
    m, k = a.shape
    ni, nj, j0 = m // bm, n // bn, row0 // bn
    assert row0 % bn == 0
    c_in, c_out, c_shape = _cast_specs(cast_src, ni * nj, lambda i, j: i * nj + j)
    return pl.pallas_call(
        _inproj_kernel,
        grid=(ni, nj),
        in_specs=[pl.BlockSpec((bm, k), lambda i, j: (i, 0)),
                  pl.BlockSpec((bn, k), lambda i, j: (j0 + j, 0)),
                  pl.BlockSpec((V7X_LANES, k), lambda i, j: (0, 0)),
                  c_in],
        out_specs=[pl.BlockSpec((bm, bn), lambda i, j: (i, j)),
                   pl.BlockSpec((bm, V7X_LANES), lambda i, j: (i, 0)),
                   c_out],
        out_shape=[jax.ShapeDtypeStruct((m, n), BF16),
                   jax.ShapeDtypeStruct((m, V7X_LANES), F32),
                   c_shape],
        compiler_params=_params(("arbitrary", "arbitrary")),
        name="in_proj_vr",
    )(a, w_t, wlr_t, cast_src)


def _matmul_kernel(a_ref, w_ref, *rest, nk, act, cast, w_t):
    if cast:
        src_ref, o_ref, dst_ref = rest[:3]
        dst_ref[...] = src_ref[...].astype(dst_ref.dtype)
    else:
        o_ref = rest[0]

    def product():
        if w_t:
            return lax.dot_general(a_ref[...], w_ref[...], _NT, preferred_element_type=F32)
        return jnp.dot(a_ref[...], w_ref[...], preferred_element_type=F32)

    def finish(r):
        if act == "relu2":
            r = jnp.square(jnp.maximum(r, 0.0))
        elif act == "gelu":
            r = _gelu(r)
        o_ref[...] = r.astype(o_ref.dtype)

    if nk == 1:
        finish(product())
        return
    acc_ref = rest[-1]
    kk = pl.program_id(2)

    @pl.when(kk == 0)
    def _():
        acc_ref[...] = product()

    @pl.when(jnp.logical_and(kk > 0, kk < nk - 1))
    def _():
        acc_ref[...] += product()

    @pl.when(kk == nk - 1)
    def _():
        finish(acc_ref[...] + product())


def _matmul(a, w, out_dtype, bm, bn, bk, act=None, cast_src=None, w_t=False, a_row0=0, name="matmul"):
    k = a.shape[1]
    m = a.shape[0] - a_row0
    n = w.shape[0] if w_t else w.shape[1]
    ni, nj, nk, i0 = m // bm, n // bn, k // bk, a_row0 // bm
    assert a_row0 % bm == 0
    w_spec = (pl.BlockSpec((bn, bk), lambda i, j, kk: (j, kk)) if w_t
              else pl.BlockSpec((bk, bn), lambda i, j, kk: (kk, j)))
    in_specs = [pl.BlockSpec((bm, bk), lambda i, j, kk: (i0 + i, kk)), w_spec]
    out_specs = [pl.BlockSpec((bm, bn), lambda i, j, kk: (i, j))]
    out_shape = [jax.ShapeDtypeStruct((m, n), out_dtype)]
    args = [a, w]
    if cast_src is not None:
        c_in, c_out, c_shape = _cast_specs(cast_src, ni * nj * nk, lambda i, j, kk: (i * nj + j) * nk + kk)
        in_specs.append(c_in)
        out_specs.append(c_out)
        out_shape.append(c_shape)
        args.append(cast_src)
    res = pl.pallas_call(
        functools.partial(_matmul_kernel, nk=nk, act=act, cast=cast_src is not None, w_t=w_t),
        grid=(ni, nj, nk),
        in_specs=in_specs,
        out_specs=out_specs,
        out_shape=out_shape,
        scratch_shapes=[pltpu.VMEM((bm, bn), F32)] if nk > 1 else [],
        compiler_params=_params(("arbitrary", "arbitrary", "arbitrary")),
        name=name,
    )(*args)
    return res if cast_src is not None else res[0]


_Dir = collections.namedtuple("_Dir", "q k v lr wd bd cmat o st reverse")


def _gla_block(dirs, emit, *, hb, dk, dv, tb):
    nch = tb // GLA_CHUNK
    nt = _NT
    tn = (((0,), (0,)), ((), ()))

    def ks(h):
        return slice(h * dk, (h + 1) * dk)

    def vs(h):
        return slice(h * dv, (h + 1) * dv)

    def rows(c):
        return slice(c * GLA_CHUNK, (c + 1) * GLA_CHUNK)

    gates = []
    for d in dirs:
        a = jnp.dot(d.lr[...].astype(BF16), d.wd[...].astype(BF16), preferred_element_type=F32) + d.bd[...]
        la = (jnp.minimum(a, 0.0) - jnp.log(1.0 + jnp.exp(-jnp.abs(a)))) * (LOG2_E / GLA_TAU)
        hi = la.astype(BF16)
        lo = (la - hi.astype(F32)).astype(BF16)
        tri = d.cmat[...]
        gates.append((jnp.dot(tri, hi, preferred_element_type=F32)
                      + jnp.dot(tri, lo, preferred_element_type=F32), tri))

    work = []
    for d, (cum, tri) in zip(dirs, gates):
        kr = d.k[...].astype(F32)
        ends = [c * GLA_CHUNK if d.reverse else (c + 1) * GLA_CHUNK - 1 for c in range(nch)]
        tot = [cum[e:e + 1] for e in ends]
        kd = [(kr[rows(c)] * jnp.exp2(tot[c] - cum[rows(c)])).astype(BF16) for c in range(nch)]
        qe = ke = None
        if emit:
            qe = (d.q[...].astype(F32) * (jnp.exp2(cum) * dk ** -0.5)).astype(BF16)
            ke = (kr * jnp.exp2(-cum)).astype(BF16)
        work.append((d, d.v[...], qe, ke, kd, tot, tri))

    heads = range(hb)
    intra = {}
    if emit:
        att = {(i, h): lax.dot_general(qe[:, ks(h)], ke[:, ks(h)], nt, preferred_element_type=F32)
               for i, (_, _, qe, ke, _, _, _) in enumerate(work) for h in heads}
        for i, (_, v, _, _, _, _, tri) in enumerate(work):
            for h in heads:
                att_m = jnp.where(tri > 0, att[i, h].astype(BF16), jnp.zeros((), BF16))
                intra[i, h] = jnp.dot(att_m, v[:, vs(h)], preferred_element_type=F32)

    upd = {(i, h, c): lax.dot_general(v[rows(c), vs(h)], kd[c][:, ks(h)], tn, preferred_element_type=F32)
           for i, (_, v, _, _, kd, _, _) in enumerate(work) for h in heads for c in range(nch)}

    seen = {}
    for i, (d, _, _, _, _, tot, _) in enumerate(work):
        order = range(nch - 1, -1, -1) if d.reverse else range(nch)
        for h in heads:
            st = d.st[h]
            for c in order:
                if emit:
                    seen[i, h, c] = st.astype(BF16)
                st = st * jnp.exp2(tot[c][:, ks(h)]) + upd[i, h, c]
            d.st[h] = st

    if emit:
        for i, (d, _, qe, _, _, _, _) in enumerate(work):
            for h in heads:
                for c in range(nch):
                    inter = lax.dot_general(qe[rows(c), ks(h)], seen[i, h, c], nt, preferred_element_type=F32)
                    d.o[rows(c), vs(h)] = (intra[i, h][rows(c)] + inter).astype(d.o.dtype)


def _gla_kernel(qf, kf, vf, lrf, qb, kb, vb, lrb,
                wdf, bdf, wdb, bdb, cmf, cmb, of_ref, ob_ref, stf, stb, *, n_ctx_blk, **kw):
    s = pl.program_id(2)
    dirs = (_Dir(qf, kf, vf, lrf, wdf, bdf, cmf, of_ref, stf, False),
            _Dir(qb, kb, vb, lrb, wdb, bdb, cmb, ob_ref, stb, True))

    @pl.when(s == 0)
    def _():
        stf[...] = jnp.zeros_like(stf)
        stb[...] = jnp.zeros_like(stb)

    @pl.when(s < n_ctx_blk)
    def _():
        _gla_block(dirs, False, **kw)

    @pl.when(s >= n_ctx_blk)
    def _():
        _gla_block(dirs, True, **kw)


def _gla(zqk, zvr, lr, wdf, bdf, wdb, bdb, cmf, cmb, *, bsz, s_len, c_len, tb, hb, dk, dv):
    n_ctx_blk, n_x_blk = c_len // tb, s_len // tb
    nb = n_ctx_blk + n_x_blk
    key_w = GLA_HEADS * dk
    qw, vw = hb * dk, hb * dv
    k_off = key_w // qw

    def fblk(s):
        return s

    def bblk(s):
        return jnp.where(s < n_ctx_blk, n_ctx_blk - 1 - s, nb - 1 - s + n_ctx_blk)

    def specs(blk):
        def row(b, s):
            return _row_block(b, blk(s), bsz, n_ctx_blk, n_x_blk)

        return [
            pl.BlockSpec((tb, qw), lambda b, g, s: (row(b, s), g)),
            pl.BlockSpec((tb, qw), lambda b, g, s: (row(b, s), k_off + g)),
            pl.BlockSpec((tb, vw), lambda b, g, s: (row(b, s), g)),
            pl.BlockSpec((tb, V7X_LANES), lambda b, g, s: (row(b, s), 0)),
        ]

    def wspec():
        return [pl.BlockSpec((V7X_LANES, qw), lambda b, g, s: (0, g)),
                pl.BlockSpec((1, qw), lambda b, g, s: (0, g))]

    def oblk_f(b, g, s):
        return (b * n_x_blk + jnp.maximum(s - n_ctx_blk, 0), g)

    def oblk_b(b, g, s):
        return (b * n_x_blk + jnp.minimum(nb - 1 - s + n_ctx_blk, nb - 1) - n_ctx_blk, g)

    cspec = pl.BlockSpec((tb, tb), lambda b, g, s: (0, 0))
    kern = functools.partial(_gla_kernel, n_ctx_blk=n_ctx_blk, hb=hb, dk=dk, dv=dv, tb=tb)
    o_shape = jax.ShapeDtypeStruct((bsz * s_len, GLA_HEADS * dv), BF16)
    return pl.pallas_call(
        kern,
        grid=(bsz, GLA_HEADS // hb, nb),
        in_specs=specs(fblk) + specs(bblk) + wspec() + wspec() + [cspec, cspec],
        out_specs=[pl.BlockSpec((tb, vw), oblk_f), pl.BlockSpec((tb, vw), oblk_b)],
        out_shape=[o_shape, o_shape],
        scratch_shapes=[pltpu.VMEM((hb, dv, dk), F32), pltpu.VMEM((hb, dv, dk), F32)],
        compiler_params=_params(("arbitrary", "arbitrary", "arbitrary")),
        name="gla_scan",
    )(zqk, zqk, zvr, lr, zqk, zqk, zvr, lr, wdf, bdf, wdb, bdb, cmf, cmb)


def _mix_kernel(of_ref, ob_ref, r_ref, u_ref, vv_ref, gn_ref, lg_ref, lb_ref, ws_ref, bs_ref, y_ref,
                *, dv, val_w, groups, tb):
    o = of_ref[...].astype(F32) + ob_ref[...].astype(F32)
    for h in range(val_w // dv):
        cs = slice(h * dv, (h + 1) * dv)
        r = r_ref[:, cs].astype(F32)
        y_ref[:, cs] = (_rms(o[:, cs], gn_ref[:, cs]) * _silu(r)).astype(y_ref.dtype)

    u = u_ref[...].astype(F32)
    vv = vv_ref[...].astype(F32)
    mu = jnp.mean(vv, axis=-1, keepdims=True)
    cen = vv - mu
    var = jnp.mean(cen * cen, axis=-1, keepdims=True)
    vn = (cen * lax.rsqrt(var + EPS) * lg_ref[...] + lb_ref[...]).astype(BF16)
    sgw = vn.shape[1]
    gw = sgw // groups
    for c in range(tb // SG_CHUNK):
        rows = slice(c * SG_CHUNK, (c + 1) * SG_CHUNK)
        for g in range(groups):
            cs = slice(g * gw, (g + 1) * gw)
            sg = jnp.dot(ws_ref[g], vn[rows, cs], preferred_element_type=F32) + bs_ref[:, cs]
            y_ref[rows, val_w + g * gw:val_w + (g + 1) * gw] = (u[rows, cs] * sg).astype(y_ref.dtype)


def _mix(o_f, o_b, z, zg, gn, lg, lb, ws, bsx, *, bsz, s_len, c_len, tb, dv, val_w, sgw, r_off):
    t = bsz * s_len
    groups = ws.shape[0]

    def zrow(i):
        return bsz * c_len // tb + i

    kern = functools.partial(_mix_kernel, dv=dv, val_w=val_w, groups=groups, tb=tb)
    return pl.pallas_call(
        kern,
        grid=(t // tb,),
        in_specs=[pl.BlockSpec((tb, val_w), lambda i: (i, 0)),
                  pl.BlockSpec((tb, val_w), lambda i: (i, 0)),
                  pl.BlockSpec((tb, val_w), lambda i: (zrow(i), r_off // val_w)),
                  pl.BlockSpec((tb, sgw), lambda i: (i, 0)),
                  pl.BlockSpec((tb, sgw), lambda i: (i, 1)),
                  pl.BlockSpec((1, val_w), lambda i: (0, 0)),
                  pl.BlockSpec((1, sgw), lambda i: (0, 0)),
                  pl.BlockSpec((1, sgw), lambda i: (0, 0)),
                  pl.BlockSpec((groups, SG_CHUNK, SG_CHUNK), lambda i: (0, 0, 0)),
                  pl.BlockSpec((SG_CHUNK, sgw), lambda i: (0, 0))],
        out_specs=pl.BlockSpec((tb, val_w + sgw), lambda i: (i, 0)),
        out_shape=jax.ShapeDtypeStruct((t, val_w + sgw), BF16),
        compiler_params=_params(("arbitrary",)),
        name="mix_readout",
    )(o_f, o_b, z, zg, zg, gn, lg, lb, ws, bsx)


def _first_residual(x_ref, m_ref, g1_ref, pg_ref):
    return x_ref[...] + g1_ref[...] * _rms(m_ref[...].astype(F32), pg_ref[...])


def _res1_kernel(x_ref, m_ref, g1_ref, sh_ref, sc_ref, pg_ref, ng_ref, h2_ref):
    x1 = _first_residual(x_ref, m_ref, g1_ref, pg_ref)
    h2_ref[...] = (_rms(x1, ng_ref[...]) * (1.0 + sc_ref[...]) + sh_ref[...]).astype(h2_ref.dtype)


def _mod_spec(n_x, d, col):
    return pl.BlockSpec((None, 1, d), lambda i: (i // n_x, 0, col))


def _res1(x2, mix, mod3, post_g, pre_g, *, s_len, tb):
    t, d = x2.shape
    n_x = s_len // tb
    row = pl.BlockSpec((tb, d), lambda i: (i, 0))
    vec = pl.BlockSpec((1, d), lambda i: (0, 0))
    return pl.pallas_call(
        _res1_kernel,
        grid=(t // tb,),
        in_specs=[row, row, _mod_spec(n_x, d, 2), _mod_spec(n_x, d, 3), _mod_spec(n_x, d, 4), vec, vec],
        out_specs=row,
        out_shape=jax.ShapeDtypeStruct((t, d), BF16),
        compiler_params=_params(("arbitrary",)),
        name="residual1",
    )(x2, mix, mod3, mod3, mod3, post_g, pre_g)


def _res2_kernel(x_ref, m1_ref, m2_ref, g1_ref, g2_ref, pg1_ref, pg2_ref, o_ref):
    x1 = _first_residual(x_ref, m1_ref, g1_ref, pg1_ref)
    o_ref[...] = x1 + g2_ref[...] * _rms(m2_ref[...].astype(F32), pg2_ref[...])


def _res2(x2, mix, m2, mod3, post1_g, post2_g, *, s_len, tb):
    t, d = x2.shape
    n_x = s_len // tb
    row = pl.BlockSpec((tb, d), lambda i: (i, 0))
    vec = pl.BlockSpec((1, d), lambda i: (0, 0))
    return pl.pallas_call(
        _res2_kernel,
        grid=(t // tb,),
        in_specs=[row, row, row, _mod_spec(n_x, d, 2), _mod_spec(n_x, d, 5), vec, vec],
        out_specs=row,
        out_shape=jax.ShapeDtypeStruct((t, d), F32),
        compiler_params=_params(("arbitrary",)),
        name="residual2",
    )(x2, mix, m2, mod3, mod3, post1_g, post2_g)


def _pair_order(t, axis, dk):
    shp = t.shape
    t = t.reshape(shp[:axis] + (shp[axis] // dk, 2, 2, dk // 4) + shp[axis + 1:])
    return jnp.swapaxes(t, axis + 1, axis + 2).reshape(shp)


def _rope_tables(s_len, c_len, dk, bsz):
    m = dk // 4
    inv_freq = ROPE_BASE ** (-jnp.arange(m, dtype=F32) / m)
    pos = jnp.arange(s_len)
    ang_r = (pos // GRID_W).astype(F32)[:, None] * inv_freq[None, :]
    ang_c = (pos % GRID_W).astype(F32)[:, None] * inv_freq[None, :]
    cos = jnp.concatenate([jnp.cos(ang_r), jnp.cos(ang_c)] * 2, axis=-1)
    sin = jnp.concatenate([-jnp.sin(ang_r), -jnp.sin(ang_c), jnp.sin(ang_r), jnp.sin(ang_c)], axis=-1)
    return (jnp.concatenate([jnp.ones((bsz * c_len, dk), F32)] + [cos] * bsz, axis=0),
            jnp.concatenate([jnp.zeros((bsz * c_len, dk), F32)] + [sin] * bsz, axis=0))


def _chunk_matrices(tb):
    i = jnp.arange(tb)
    same = (i[:, None] // GLA_CHUNK) == (i[None, :] // GLA_CHUNK)
    return ((same & (i[None, :] <= i[:, None])).astype(BF16),
            (same & (i[None, :] >= i[:, None])).astype(BF16))


def _tile(n, pref):
    return pref if n % pref == 0 else n


def kernel(x, c, ctx, c_ctx, w_ada, b_ada, pre1_g, post1_g, pre2_g, post2_g, w_in, w_dec_f, b_dec_f,
           w_dec_b, b_dec_b, gla_norm_g, sg_ln_g, sg_ln_b, w_s, b_s, w_o, w_1, w_2):
    bsz, s_len, d = x.shape
    c_len = ctx.shape[1]
    assert w_ada.shape[0] == 1, "single-layer block only"
    assert bsz < MOD_ROWS
    lowrank, key_w = w_dec_f.shape[1], w_dec_f.shape[2]
    dk = key_w // GLA_HEADS
    dv = gla_norm_g.shape[2]
    val_w = GLA_HEADS * dv
    sgw = sg_ln_g.shape[1]
    lf0 = 2 * key_w + 2 * val_w
    sg0 = lf0 + 2 * lowrank
    assert dk == V7X_LANES and 2 * lowrank <= V7X_LANES and w_in.shape[2] == sg0 + 2 * sgw
    t = bsz * s_len

    w_in_t = jnp.swapaxes(w_in[0], 0, 1).astype(BF16)
    w_sg_t = w_in_t[sg0:]
    w_qk_t = _pair_order(w_in_t[:2 * key_w], 0, dk)
    w_lr_t = jnp.pad(w_in_t[lf0:sg0], ((0, V7X_LANES - 2 * lowrank), (0, 0)))
    wdf = jnp.pad(_pair_order(w_dec_f[0], 1, dk), ((0, V7X_LANES - lowrank), (0, 0)))
    wdb = jnp.pad(_pair_order(w_dec_b[0], 1, dk), ((lowrank, V7X_LANES - 2 * lowrank), (0, 0)))
    bdf, bdb = _pair_order(b_dec_f, 1, dk), _pair_order(b_dec_b, 1, dk)

    c_all = jnp.concatenate([c, c_ctx[None], jnp.zeros((MOD_ROWS - bsz - 1, d), F32)], axis=0)
    mod = _ada(c_all, w_ada[0], b_ada[0])
    mod3 = mod.reshape(MOD_ROWS, 1, N_MOD * d)

    rb = _tile(math.gcd(s_len, c_len), 256)
    hx = _prenorm(x, ctx, mod3, pre1_g, rb)

    m_all = hx.shape[0]
    bm_all = _tile(m_all, 1024)
    cos_t, sin_t = _rope_tables(s_len, c_len, dk, bsz)
    zqk = _qkproj(hx, w_qk_t, cos_t, sin_t, bm_all, _tile(2 * key_w, 1024), dk=dk)
    zvr, lr, w_o_b = _inproj(hx, w_in_t, 2 * key_w, 2 * val_w, w_lr_t, w_o[0], bm_all,
                             _tile(math.gcd(2 * key_w, 2 * val_w), 1024))
    zg = _matmul(hx, w_sg_t, BF16, _tile(math.gcd(bsz * c_len, t), 1024), _tile(2 * sgw, 1024), d, act="gelu",
                 w_t=True, a_row0=bsz * c_len, name="in_proj_sg")

    cmf, cmb = _chunk_matrices(rb)
    o_f, o_b = _gla(zqk, zvr, lr, wdf, bdf, wdb, bdb, cmf, cmb,
                    bsz=bsz, s_len=s_len, c_len=c_len, tb=rb, hb=GLA_HEADS, dk=dk, dv=dv)

    bsx = jnp.repeat(b_s[0].T, sgw // w_s.shape[1], axis=1)
    y = _mix(o_f, o_b, zvr, zg, gla_norm_g.reshape(1, val_w), sg_ln_g, sg_ln_b, w_s[0].astype(BF16), bsx,
             bsz=bsz, s_len=s_len, c_len=c_len, tb=rb, dv=dv, val_w=val_w, sgw=sgw, r_off=val_w)

    bm = _tile(t, 1024)
    mix, w_1_b = _matmul(y, w_o_b, BF16, bm, _tile(d, 1024), d, cast_src=w_1[0], name="out_proj")
    x2 = x.reshape(t, d)
    h2 = _res1(x2, mix, mod3, post1_g, pre2_g, s_len=s_len, tb=rb)
    d_ff = w_1_b.shape[1]
    hmid, w_2_b = _matmul(h2, w_1_b, BF16, bm, _tile(d_ff, 1024), d, act="relu2", cast_src=w_2[0],
                          name="mlp_up")
    m2 = _matmul(hmid, w_2_b, BF16, bm, _tile(d, 1024), _tile(d_ff, 4096), name="mlp_down")
    out = _res2(x2, mix, m2, mod3, post1_g, post2_g, s_len=s_len, tb=rb)
    return out.reshape(bsz, s_len, d)
```

```python
import collections
import functools
import math

import jax
import jax.numpy as jnp
from jax import lax
from jax.experimental import pallas as pl
from jax.experimental.pallas import tpu as pltpu

GRID_W = 64
GLA_HEADS = 8
GLA_TAU = 16.0
ROPE_BASE = 10000.0
N_MOD = 6
EPS = 1e-6
SG_CHUNK = 128
GLA_CHUNK = 64
LOG2_E = math.log2(math.e)

V7X_LANES = 128
MOD_ROWS = 8
VMEM_LIMIT_BYTES = 56 * 1024 * 1024

F32 = jnp.float32
BF16 = jnp.bfloat16


def _params(sem, vmem=VMEM_LIMIT_BYTES):
    return pltpu.CompilerParams(dimension_semantics=sem, vmem_limit_bytes=vmem)


def _rms(t, g):
    return t * lax.rsqrt(jnp.mean(t * t, axis=-1, keepdims=True) + EPS) * g


def _silu(t):
    return t * (1.0 / (1.0 + jnp.exp(-t)))


def _gelu(t):
    return 0.5 * t * (1.0 + lax.erf(t * (2.0 ** -0.5)))


def _ada_kernel(c_ref, w_ref, b_ref, o_ref):
    cond = _silu(c_ref[...])
    o_ref[...] = jnp.dot(cond.astype(BF16), w_ref[...].astype(BF16),
                         preferred_element_type=F32) + b_ref[...]


def _ada(c_all, w_ada, b_ada):
    d, n = w_ada.shape
    bn = min(n, 1024)
    return pl.pallas_call(
        _ada_kernel,
        grid=(n // bn,),
        in_specs=[pl.BlockSpec((MOD_ROWS, d), lambda j: (0, 0)),
                  pl.BlockSpec((d, bn), lambda j: (0, j)),
                  pl.BlockSpec((1, bn), lambda j: (0, j))],
        out_specs=pl.BlockSpec((MOD_ROWS, bn), lambda j: (0, j)),
        out_shape=jax.ShapeDtypeStruct((MOD_ROWS, n), F32),
        compiler_params=_params(("arbitrary",)),
        name="ada_mod",
    )(c_all, w_ada, b_ada.reshape(1, n))


def _modulated_norm(t, mod_ref, g_ref):
    d = t.shape[1]
    return _rms(t, g_ref[...]) * (1.0 + mod_ref[:, d:2 * d]) + mod_ref[:, 0:d]


def _ctx_prenorm_kernel(ctx_ref, mod_ref, g_ref, o_ref):
    o_ref[...] = _modulated_norm(ctx_ref[...], mod_ref, g_ref).astype(o_ref.dtype)


def _ctx_prenorm(ctx2, mod3, g, mod_row, rb):
    m, d = ctx2.shape
    return pl.pallas_call(
        _ctx_prenorm_kernel,
        grid=(m // rb,),
        in_specs=[pl.BlockSpec((rb, d), lambda i: (i, 0)),
                  pl.BlockSpec((None, 1, 2 * d), lambda i: (mod_row, 0, 0)),
                  pl.BlockSpec((1, d), lambda i: (0, 0))],
        out_specs=pl.BlockSpec((rb, d), lambda i: (i, 0)),
        out_shape=jax.ShapeDtypeStruct((m, d), BF16),
        compiler_params=_params(("arbitrary",)),
        name="ctx_prenorm",
    )(ctx2, mod3, g)


def _row_block(b, blk, bsz, n_ctx_blk, n_x_blk):
    return jnp.where(blk < n_ctx_blk, b * n_ctx_blk + blk, bsz * n_ctx_blk + b * n_x_blk + blk - n_ctx_blk)


CAST_ROWS = 64
PRENORM_ROWS = 128


def _cast_specs(src, nsteps, lin):
    rows, cols = src.shape
    cr = CAST_ROWS
    while rows % cr or rows // cr > nsteps:
        cr += CAST_ROWS
    nblk = rows // cr
    spec = pl.BlockSpec((cr, cols), lambda *g: (jnp.minimum(lin(*g), nblk - 1), 0))
    return spec, spec, jax.ShapeDtypeStruct((rows, cols), BF16)


_NT = (((1,), (1,)), ((), ()))


def _qkproj_kernel(x_ref, hc_ref, mod_ref, g_ref, w_ref, cos_ref, sin_ref, z_ref, hx_ref, *, dk, n_ctx_tiles):
    i = pl.program_id(0)

    @pl.when(pl.program_id(1) == 0)
    def _():
        @pl.when(i < n_ctx_tiles)
        def _():
            hx_ref[...] = hc_ref[...]

        @pl.when(i >= n_ctx_tiles)
        def _():
            for r0 in range(0, hx_ref.shape[0], PRENORM_ROWS):
                rows = slice(r0, r0 + PRENORM_ROWS)
                hx_ref[rows, :] = _modulated_norm(x_ref[rows, :], mod_ref, g_ref).astype(hx_ref.dtype)

    r = lax.dot_general(hx_ref[...], w_ref[...], _NT, preferred_element_type=F32)
    cos, sin = cos_ref[...], sin_ref[...]
    for h in range(r.shape[1] // dk):
        cs = slice(h * dk, (h + 1) * dk)
        z_ref[:, cs] = (r[:, cs] * cos + pltpu.roll(r[:, cs], dk // 2, 1) * sin).astype(z_ref.dtype)


def _qkproj(x2, hc, mod3, g, w_t, cos_t, sin_t, bm, bn, *, dk, s_len):
    t, d = x2.shape
    m_ctx = hc.shape[0]
    n = w_t.shape[0]
    n_ctx_tiles, tiles_per_batch = m_ctx // bm, s_len // bm
    assert m_ctx % bm == 0 and s_len % bm == 0

    def lat(i):
        return jnp.maximum(i - n_ctx_tiles, 0)

    return pl.pallas_call(
        functools.partial(_qkproj_kernel, dk=dk, n_ctx_tiles=n_ctx_tiles),
        grid=((m_ctx + t) // bm, n // bn),
        in_specs=[pl.BlockSpec((bm, d), lambda i, j: (lat(i), 0)),
                  pl.BlockSpec((bm, d), lambda i, j: (jnp.minimum(i, n_ctx_tiles - 1), 0)),
                  pl.BlockSpec((None, 1, 2 * d), lambda i, j: (lat(i) // tiles_per_batch, 0, 0)),
                  pl.BlockSpec((1, d), lambda i, j: (0, 0)),
                  pl.BlockSpec((bn, d), lambda i, j: (j, 0)),
                  pl.BlockSpec((bm, dk), lambda i, j: (i, 0)),
                  pl.BlockSpec((bm, dk), lambda i, j: (i, 0))],
        out_specs=[pl.BlockSpec((bm, bn), lambda i, j: (i, j)),
                   pl.BlockSpec((bm, d), lambda i, j: (i, 0))],
        out_shape=[jax.ShapeDtypeStruct((m_ctx + t, n), BF16),
                   jax.ShapeDtypeStruct((m_ctx + t, d), BF16)],
        compiler_params=_params(("arbitrary", "arbitrary"), VMEM_LIMIT_BYTES + 3 * 1024 * 1024),
        name="in_proj_qk",
    )(x2, hc, mod3, g, w_t, cos_t, sin_t)


def _inproj_kernel(a_ref, w_ref, wlr_ref, src_ref, z_ref, lr_ref, dst_ref):
    a = a_ref[...]
    z_ref[...] = lax.dot_general(a, w_ref[...], _NT, preferred_element_type=F32).astype(z_ref.dtype)
    dst_ref[...] = src_ref[...].astype(dst_ref.dtype)

    @pl.when(pl.program_id(1) == 0)
    def _():
        lr_ref[...] = lax.dot_general(a, wlr_ref[...], _NT, preferred_element_type=F32)


def _inproj(a, w_t, row0, n, wlr_t, cast_src, bm, bn):
    m, k = a.shape
    ni, nj, j0 = m // bm, n // bn, row0 // bn
    assert row0 % bn == 0
    c_in, c_out, c_shape = _cast_specs(cast_src, ni * nj, lambda i, j: i * nj + j)
    return pl.pallas_call(
        _inproj_kernel,
        grid=(ni, nj),
        in_specs=[pl.BlockSpec((bm, k), lambda i, j: (i, 0)),
                  pl.BlockSpec((bn, k), lambda i, j: (j0 + j, 0)),
                  pl.BlockSpec((V7X_LANES, k), lambda i, j: (0, 0)),
                  c_in],
        out_specs=[pl.BlockSpec((bm, bn), lambda i, j: (i, j)),
                   pl.BlockSpec((bm, V7X_LANES), lambda i, j: (i, 0)),
                   c_out],
        out_shape=[jax.ShapeDtypeStruct((m, n), BF16),
                   jax.ShapeDtypeStruct((m, V7X_LANES), F32),
                   c_shape],
        compiler_params=_params(("arbitrary", "arbitrary")),
        name="in_proj_vr",
    )(a, w_t, wlr_t, cast_src)


def _matmul_kernel(a_ref, w_ref, *rest, nk, act, cast, w_t):
    if cast:
        src_ref, o_ref, dst_ref = rest[:3]
        dst_ref[...] = src_ref[...].astype(dst_ref.dtype)
    else:
        o_ref = rest[0]

    def product():
        if w_t:
            return lax.dot_general(a_ref[...], w_ref[...], _NT, preferred_element_type=F32)
        return jnp.dot(a_ref[...], w_ref[...], preferred_element_type=F32)

    def finish(r):
        if act == "relu2":
            r = jnp.square(jnp.maximum(r, 0.0))
        elif act == "gelu":
            r = _gelu(r)
        o_ref[...] = r.astype(o_ref.dtype)

    if nk == 1:
        finish(product())
        return
    acc_ref = rest[-1]
    kk = pl.program_id(2)

    @pl.when(kk == 0)
    def _():
        acc_ref[...] = product()

    @pl.when(jnp.logical_and(kk > 0, kk < nk - 1))
    def _():
        acc_ref[...] += product()

    @pl.when(kk == nk - 1)
    def _():
        finish(acc_ref[...] + product())


def _matmul(a, w, out_dtype, bm, bn, bk, act=None, cast_src=None, w_t=False, a_row0=0, name="matmul"):
    k = a.shape[1]
    m = a.shape[0] - a_row0
    n = w.shape[0] if w_t else w.shape[1]
    ni, nj, nk, i0 = m // bm, n // bn, k // bk, a_row0 // bm
    assert a_row0 % bm == 0
    w_spec = (pl.BlockSpec((bn, bk), lambda i, j, kk: (j, kk)) if w_t
              else pl.BlockSpec((bk, bn), lambda i, j, kk: (kk, j)))
    in_specs = [pl.BlockSpec((bm, bk), lambda i, j, kk: (i0 + i, kk)), w_spec]
    out_specs = [pl.BlockSpec((bm, bn), lambda i, j, kk: (i, j))]
    out_shape = [jax.ShapeDtypeStruct((m, n), out_dtype)]
    args = [a, w]
    if cast_src is not None:
        c_in, c_out, c_shape = _cast_specs(cast_src, ni * nj * nk, lambda i, j, kk: (i * nj + j) * nk + kk)
        in_specs.append(c_in)
        out_specs.append(c_out)
        out_shape.append(c_shape)
        args.append(cast_src)
    res = pl.pallas_call(
        functools.partial(_matmul_kernel, nk=nk, act=act, cast=cast_src is not None, w_t=w_t),
        grid=(ni, nj, nk),
        in_specs=in_specs,
        out_specs=out_specs,
        out_shape=out_shape,
        scratch_shapes=[pltpu.VMEM((bm, bn), F32)] if nk > 1 else [],
        compiler_params=_params(("arbitrary", "arbitrary", "arbitrary")),
        name=name,
    )(*args)
    return res if cast_src is not None else res[0]


_Dir = collections.namedtuple("_Dir", "q k v lr wd bd cmat o st reverse")


def _gla_block(dirs, emit, *, hb, dk, dv, tb):
    nch = tb // GLA_CHUNK
    nt = _NT
    tn = (((0,), (0,)), ((), ()))

    def ks(h):
        return slice(h * dk, (h + 1) * dk)

    def vs(h):
        return slice(h * dv, (h + 1) * dv)

    def rows(c):
        return slice(c * GLA_CHUNK, (c + 1) * GLA_CHUNK)

    gates = []
    for d in dirs:
        a = jnp.dot(d.lr[...].astype(BF16), d.wd[...].astype(BF16), preferred_element_type=F32) + d.bd[...]
        la = (jnp.minimum(a, 0.0) - jnp.log(1.0 + jnp.exp(-jnp.abs(a)))) * (LOG2_E / GLA_TAU)
        hi = la.astype(BF16)
        lo = (la - hi.astype(F32)).astype(BF16)
        tri = d.cmat[...]
        gates.append((jnp.dot(tri, hi, preferred_element_type=F32)
                      + jnp.dot(tri, lo, preferred_element_type=F32), tri))

    work = []
    for d, (cum, tri) in zip(dirs, gates):
        kr = d.k[...].astype(F32)
        ends = [c * GLA_CHUNK if d.reverse else (c + 1) * GLA_CHUNK - 1 for c in range(nch)]
        tot = [cum[e:e + 1] for e in ends]
        kd = [(kr[rows(c)] * jnp.exp2(tot[c] - cum[rows(c)])).astype(BF16) for c in range(nch)]
        qe = ke = None
        if emit:
            qe = (d.q[...].astype(F32) * (jnp.exp2(cum) * dk ** -0.5)).astype(BF16)
            ke = (kr * jnp.exp2(-cum)).astype(BF16)
        work.append((d, d.v[...], qe, ke, kd, tot, tri))

    heads = range(hb)
    intra = {}
    if emit:
        att = {(i, h): lax.dot_general(qe[:, ks(h)], ke[:, ks(h)], nt, preferred_element_type=F32)
               for i, (_, _, qe, ke, _, _, _) in enumerate(work) for h in heads}
        for i, (_, v, _, _, _, _, tri) in enumerate(work):
            for h in heads:
                att_m = jnp.where(tri > 0, att[i, h].astype(BF16), jnp.zeros((), BF16))
                intra[i, h] = jnp.dot(att_m, v[:, vs(h)], preferred_element_type=F32)

    upd = {(i, h, c): lax.dot_general(v[rows(c), vs(h)], kd[c][:, ks(h)], tn, preferred_element_type=F32)
           for i, (_, v, _, _, kd, _, _) in enumerate(work) for h in heads for c in range(nch)}

    seen = {}
    for i, (d, _, _, _, _, tot, _) in enumerate(work):
        order = range(nch - 1, -1, -1) if d.reverse else range(nch)
        for h in heads:
            st = d.st[h]
            for c in order:
                if emit:
                    seen[i, h, c] = st.astype(BF16)
                st = st * jnp.exp2(tot[c][:, ks(h)]) + upd[i, h, c]
            d.st[h] = st

    if emit:
        for i, (d, _, qe, _, _, _, _) in enumerate(work):
            for h in heads:
                for c in range(nch):
                    inter = lax.dot_general(qe[rows(c), ks(h)], seen[i, h, c], nt, preferred_element_type=F32)
                    d.o[rows(c), vs(h)] = (intra[i, h][rows(c)] + inter).astype(d.o.dtype)


def _gla_kernel(qf, kf, vf, lrf, qb, kb, vb, lrb,
                wdf, bdf, wdb, bdb, cmf, cmb, of_ref, ob_ref, stf, stb, *, n_ctx_blk, **kw):
    s = pl.program_id(2)
    dirs = (_Dir(qf, kf, vf, lrf, wdf, bdf, cmf, of_ref, stf, False),
            _Dir(qb, kb, vb, lrb, wdb, bdb, cmb, ob_ref, stb, True))

    @pl.when(s == 0)
    def _():
        stf[...] = jnp.zeros_like(stf)
        stb[...] = jnp.zeros_like(stb)

    @pl.when(s < n_ctx_blk)
    def _():
        _gla_block(dirs, False, **kw)

    @pl.when(s >= n_ctx_blk)
    def _():
        _gla_block(dirs, True, **kw)


def _gla(zqk, zvr, lr, wdf, bdf, wdb, bdb, cmf, cmb, *, bsz, s_len, c_len, tb, hb, dk, dv):
    n_ctx_blk, n_x_blk = c_len // tb, s_len // tb
    nb = n_ctx_blk + n_x_blk
    key_w = GLA_HEADS * dk
    qw, vw = hb * dk, hb * dv
    k_off = key_w // qw

    def fblk(s):
        return s

    def bblk(s):
        return jnp.where(s < n_ctx_blk, n_ctx_blk - 1 - s, nb - 1 - s + n_ctx_blk)

    def specs(blk):
        def row(b, s):
            return _row_block(b, blk(s), bsz, n_ctx_blk, n_x_blk)

        return [
            pl.BlockSpec((tb, qw), lambda b, g, s: (row(b, s), g)),
            pl.BlockSpec((tb, qw), lambda b, g, s: (row(b, s), k_off + g)),
            pl.BlockSpec((tb, vw), lambda b, g, s: (row(b, s), g)),
            pl.BlockSpec((tb, V7X_LANES), lambda b, g, s: (row(b, s), 0)),
        ]

    def wspec():
        return [pl.BlockSpec((V7X_LANES, qw), lambda b, g, s: (0, g)),
                pl.BlockSpec((1, qw), lambda b, g, s: (0, g))]

    def oblk_f(b, g, s):
        return (b * n_x_blk + jnp.maximum(s - n_ctx_blk, 0), g)

    def oblk_b(b, g, s):
        return (b * n_x_blk + jnp.minimum(nb - 1 - s + n_ctx_blk, nb - 1) - n_ctx_blk, g)

    cspec = pl.BlockSpec((tb, tb), lambda b, g, s: (0, 0))
    kern = functools.partial(_gla_kernel, n_ctx_blk=n_ctx_blk, hb=hb, dk=dk, dv=dv, tb=tb)
    o_shape = jax.ShapeDtypeStruct((bsz * s_len, GLA_HEADS * dv), BF16)
    return pl.pallas_call(
        kern,
        grid=(bsz, GLA_HEADS // hb, nb),
        in_specs=specs(fblk) + specs(bblk) + wspec() + wspec() + [cspec, cspec],
        out_specs=[pl.BlockSpec((tb, vw), oblk_f), pl.BlockSpec((tb, vw), oblk_b)],
        out_shape=[o_shape, o_shape],
        scratch_shapes=[pltpu.VMEM((hb, dv, dk), F32), pltpu.VMEM((hb, dv, dk), F32)],
        compiler_params=_params(("arbitrary", "arbitrary", "arbitrary")),
        name="gla_scan",
    )(zqk, zqk, zvr, lr, zqk, zqk, zvr, lr, wdf, bdf, wdb, bdb, cmf, cmb)


def _mix_kernel(of_ref, ob_ref, r_ref, u_ref, vv_ref, gn_ref, lg_ref, lb_ref, ws_ref, bs_ref, y_ref,
                *, dv, val_w, groups, tb):
    o = of_ref[...].astype(F32) + ob_ref[...].astype(F32)
    for h in range(val_w // dv):
        cs = slice(h * dv, (h + 1) * dv)
        r = r_ref[:, cs].astype(F32)
        y_ref[:, cs] = (_rms(o[:, cs], gn_ref[:, cs]) * _silu(r)).astype(y_ref.dtype)

    u = u_ref[...].astype(F32)
    vv = vv_ref[...].astype(F32)
    mu = jnp.mean(vv, axis=-1, keepdims=True)
    cen = vv - mu
    var = jnp.mean(cen * cen, axis=-1, keepdims=True)
    vn = (cen * lax.rsqrt(var + EPS) * lg_ref[...] + lb_ref[...]).astype(BF16)
    sgw = vn.shape[1]
    gw = sgw // groups
    for c in range(tb // SG_CHUNK):
        rows = slice(c * SG_CHUNK, (c + 1) * SG_CHUNK)
        for g in range(groups):
            cs = slice(g * gw, (g + 1) * gw)
            sg = jnp.dot(ws_ref[g], vn[rows, cs], preferred_element_type=F32) + bs_ref[:, cs]
            y_ref[rows, val_w + g * gw:val_w + (g + 1) * gw] = (u[rows, cs] * sg).astype(y_ref.dtype)


def _mix(o_f, o_b, z, zg, gn, lg, lb, ws, bsx, *, bsz, s_len, c_len, tb, dv, val_w, sgw, r_off):
    t = bsz * s_len
    groups = ws.shape[0]

    def zrow(i):
        return bsz * c_len // tb + i

    kern = functools.partial(_mix_kernel, dv=dv, val_w=val_w, groups=groups, tb=tb)
    return pl.pallas_call(
        kern,
        grid=(t // tb,),
        in_specs=[pl.BlockSpec((tb, val_w), lambda i: (i, 0)),
                  pl.BlockSpec((tb, val_w), lambda i: (i, 0)),
                  pl.BlockSpec((tb, val_w), lambda i: (zrow(i), r_off // val_w)),
                  pl.BlockSpec((tb, sgw), lambda i: (i, 0)),
                  pl.BlockSpec((tb, sgw), lambda i: (i, 1)),
                  pl.BlockSpec((1, val_w), lambda i: (0, 0)),
                  pl.BlockSpec((1, sgw), lambda i: (0, 0)),
                  pl.BlockSpec((1, sgw), lambda i: (0, 0)),
                  pl.BlockSpec((groups, SG_CHUNK, SG_CHUNK), lambda i: (0, 0, 0)),
                  pl.BlockSpec((SG_CHUNK, sgw), lambda i: (0, 0))],
        out_specs=pl.BlockSpec((tb, val_w + sgw), lambda i: (i, 0)),
        out_shape=jax.ShapeDtypeStruct((t, val_w + sgw), BF16),
        compiler_params=_params(("arbitrary",)),
        name="mix_readout",
    )(o_f, o_b, z, zg, zg, gn, lg, lb, ws, bsx)


def _res1_kernel(x_ref, m_ref, g1_ref, sh_ref, sc_ref, pg_ref, ng_ref, x1_ref, h2_ref):
    x1 = x_ref[...] + g1_ref[...] * _rms(m_ref[...].astype(F32), pg_ref[...])
    x1_ref[...] = x1
    h2_ref[...] = (_rms(x1, ng_ref[...]) * (1.0 + sc_ref[...]) + sh_ref[...]).astype(h2_ref.dtype)


def _mod_spec(n_x, d, col):
    return pl.BlockSpec((None, 1, d), lambda i: (i // n_x, 0, col))


def _res1(x2, mix, mod3, post_g, pre_g, *, s_len, tb):
    t, d = x2.shape
    n_x = s_len // tb
    row = pl.BlockSpec((tb, d), lambda i: (i, 0))
    vec = pl.BlockSpec((1, d), lambda i: (0, 0))
    return pl.pallas_call(
        _res1_kernel,
        grid=(t // tb,),
        in_specs=[row, row, _mod_spec(n_x, d, 2), _mod_spec(n_x, d, 3), _mod_spec(n_x, d, 4), vec, vec],
        out_specs=[row, row],
        out_shape=[jax.ShapeDtypeStruct((t, d), F32), jax.ShapeDtypeStruct((t, d), BF16)],
        compiler_params=_params(("arbitrary",)),
        name="residual1",
    )(x2, mix, mod3, mod3, mod3, post_g, pre_g)


def _res2_kernel(x_ref, m_ref, g2_ref, pg_ref, o_ref):
    o_ref[...] = x_ref[...] + g2_ref[...] * _rms(m_ref[...].astype(F32), pg_ref[...])


def _res2(x1, m2, mod3, post_g, *, s_len, tb):
    t, d = x1.shape
    n_x = s_len // tb
    row = pl.BlockSpec((tb, d), lambda i: (i, 0))
    return pl.pallas_call(
        _res2_kernel,
        grid=(t // tb,),
        in_specs=[row, row, _mod_spec(n_x, d, 5), pl.BlockSpec((1, d), lambda i: (0, 0))],
        out_specs=row,
        out_shape=jax.ShapeDtypeStruct((t, d), F32),
        compiler_params=_params(("arbitrary",)),
        name="residual2",
    )(x1, m2, mod3, post_g)


def _pair_order(t, axis, dk):
    shp = t.shape
    t = t.reshape(shp[:axis] + (shp[axis] // dk, 2, 2, dk // 4) + shp[axis + 1:])
    return jnp.swapaxes(t, axis + 1, axis + 2).reshape(shp)


def _rope_tables(s_len, c_len, dk, bsz):
    m = dk // 4
    inv_freq = ROPE_BASE ** (-jnp.arange(m, dtype=F32) / m)
    pos = jnp.arange(s_len)
    ang_r = (pos // GRID_W).astype(F32)[:, None] * inv_freq[None, :]
    ang_c = (pos % GRID_W).astype(F32)[:, None] * inv_freq[None, :]
    cos = jnp.concatenate([jnp.cos(ang_r), jnp.cos(ang_c)] * 2, axis=-1)
    sin = jnp.concatenate([-jnp.sin(ang_r), -jnp.sin(ang_c), jnp.sin(ang_r), jnp.sin(ang_c)], axis=-1)
    return (jnp.concatenate([jnp.ones((bsz * c_len, dk), F32)] + [cos] * bsz, axis=0),
            jnp.concatenate([jnp.zeros((bsz * c_len, dk), F32)] + [sin] * bsz, axis=0))


def _chunk_matrices(tb):
    i = jnp.arange(tb)
    same = (i[:, None] // GLA_CHUNK) == (i[None, :] // GLA_CHUNK)
    return ((same & (i[None, :] <= i[:, None])).astype(BF16),
            (same & (i[None, :] >= i[:, None])).astype(BF16))


def _tile(n, pref):
    return pref if n % pref == 0 else n


def kernel(x, c, ctx, c_ctx, w_ada, b_ada, pre1_g, post1_g, pre2_g, post2_g, w_in, w_dec_f, b_dec_f,
           w_dec_b, b_dec_b, gla_norm_g, sg_ln_g, sg_ln_b, w_s, b_s, w_o, w_1, w_2):
    bsz, s_len, d = x.shape
    c_len = ctx.shape[1]
    assert w_ada.shape[0] == 1, "single-layer block only"
    assert bsz < MOD_ROWS
    lowrank, key_w = w_dec_f.shape[1], w_dec_f.shape[2]
    dk = key_w // GLA_HEADS
    dv = gla_norm_g.shape[2]
    val_w = GLA_HEADS * dv
    sgw = sg_ln_g.shape[1]
    lf0 = 2 * key_w + 2 * val_w
    sg0 = lf0 + 2 * lowrank
    assert dk == V7X_LANES and 2 * lowrank <= V7X_LANES and w_in.shape[2] == sg0 + 2 * sgw
    t = bsz * s_len

    w_in_t = jnp.swapaxes(w_in[0], 0, 1).astype(BF16)
    w_sg_t = w_in_t[sg0:]
    w_qk_t = _pair_order(w_in_t[:2 * key_w], 0, dk)
    w_lr_t = jnp.pad(w_in_t[lf0:sg0], ((0, V7X_LANES - 2 * lowrank), (0, 0)))
    wdf = jnp.pad(_pair_order(w_dec_f[0], 1, dk), ((0, V7X_LANES - lowrank), (0, 0)))
    wdb = jnp.pad(_pair_order(w_dec_b[0], 1, dk), ((lowrank, V7X_LANES - 2 * lowrank), (0, 0)))
    bdf, bdb = _pair_order(b_dec_f, 1, dk), _pair_order(b_dec_b, 1, dk)

    c_all = jnp.concatenate([c, c_ctx[None], jnp.zeros((MOD_ROWS - bsz - 1, d), F32)], axis=0)
    mod = _ada(c_all, w_ada[0], b_ada[0])
    mod3 = mod.reshape(MOD_ROWS, 1, N_MOD * d)

    rb = _tile(math.gcd(s_len, c_len), 256)
    x2 = x.reshape(t, d)
    hc = _ctx_prenorm(ctx.reshape(bsz * c_len, d), mod3, pre1_g, bsz, rb)

    m_all = bsz * c_len + t
    bm_all = _tile(m_all, 1024)
    cos_t, sin_t = _rope_tables(s_len, c_len, dk, bsz)
    zqk, hx = _qkproj(x2, hc, mod3, pre1_g, w_qk_t, cos_t, sin_t, _tile(math.gcd(bsz * c_len, s_len), 512),
                      _tile(2 * key_w, 1024), dk=dk, s_len=s_len)
    zvr, lr, w_o_b = _inproj(hx, w_in_t, 2 * key_w, 2 * val_w, w_lr_t, w_o[0], bm_all,
                             _tile(math.gcd(2 * key_w, 2 * val_w), 1024))
    zg = _matmul(hx, w_sg_t, BF16, _tile(math.gcd(bsz * c_len, t), 1024), _tile(2 * sgw, 1024), d, act="gelu",
                 w_t=True, a_row0=bsz * c_len, name="in_proj_sg")

    cmf, cmb = _chunk_matrices(rb)
    o_f, o_b = _gla(zqk, zvr, lr, wdf, bdf, wdb, bdb, cmf, cmb,
                    bsz=bsz, s_len=s_len, c_len=c_len, tb=rb, hb=GLA_HEADS, dk=dk, dv=dv)

    bsx = jnp.repeat(b_s[0].T, sgw // w_s.shape[1], axis=1)
    y = _mix(o_f, o_b, zvr, zg, gla_norm_g.reshape(1, val_w), sg_ln_g, sg_ln_b, w_s[0].astype(BF16), bsx,
             bsz=bsz, s_len=s_len, c_len=c_len, tb=rb, dv=dv, val_w=val_w, sgw=sgw, r_off=val_w)

    bm = _tile(t, 1024)
    mix, w_1_b = _matmul(y, w_o_b, BF16, bm, _tile(d, 1024), d, cast_src=w_1[0], name="out_proj")
    x1, h2 = _res1(x2, mix, mod3, post1_g, pre2_g, s_len=s_len, tb=rb)
    d_ff = w_1_b.shape[1]
    hmid, w_2_b = _matmul(h2, w_1_b, BF16, bm, _tile(d_ff, 1024), d, act="relu2", cast_src=w_2[0],
                          name="mlp_up")
    m2 = _matmul(hmid, w_2_b, BF16, bm, _tile(d, 1024), _tile(d_ff, 4096), name="mlp_down")
    out = _res2(x1, m2, mod3, post2_g, s_len=s_len, tb=rb)
    return out.reshape(bsz, s_len, d)
```

```python
import collections
import functools
import math

import jax
import jax.numpy as jnp
from jax import lax
from jax.experimental import pallas as pl
from jax.experimental.pallas import tpu as pltpu

GRID_W = 64
GLA_HEADS = 8
GLA_TAU = 16.0
ROPE_BASE = 10000.0
N_MOD = 6
EPS = 1e-6
SG_CHUNK = 128
GLA_CHUNK = 64
LOG2_E = math.log2(math.e)

V7X_LANES = 128
MOD_ROWS = 8
VMEM_LIMIT_BYTES = 56 * 1024 * 1024
MATMUL_TILE = 1024
MLP_DOWN_K_TILE = 4096
ROW_BLOCK = 256

F32 = jnp.float32
BF16 = jnp.bfloat16


def _params(sem, vmem=VMEM_LIMIT_BYTES):
    return pltpu.CompilerParams(dimension_semantics=sem, vmem_limit_bytes=vmem)


def _rms(t, g):
    return t * lax.rsqrt(jnp.mean(t * t, axis=-1, keepdims=True) + EPS) * g


def _silu(t):
    return t * (1.0 / (1.0 + jnp.exp(-t)))


def _gelu(t):
    return 0.5 * t * (1.0 + lax.erf(t * (2.0 ** -0.5)))


def _ada_kernel(c_ref, w_ref, b_ref, o_ref):
    cond = _silu(c_ref[...])
    o_ref[...] = jnp.dot(cond.astype(BF16), w_ref[...].astype(BF16),
                         preferred_element_type=F32) + b_ref[...]


def _ada(c_all, w_ada, b_ada):
    d, n = w_ada.shape
    bn = _tile(n, MATMUL_TILE)
    return pl.pallas_call(
        _ada_kernel,
        grid=(n // bn,),
        in_specs=[pl.BlockSpec((MOD_ROWS, d), lambda j: (0, 0)),
                  pl.BlockSpec((d, bn), lambda j: (0, j)),
                  pl.BlockSpec((1, bn), lambda j: (0, j))],
        out_specs=pl.BlockSpec((MOD_ROWS, bn), lambda j: (0, j)),
        out_shape=jax.ShapeDtypeStruct((MOD_ROWS, n), F32),
        compiler_params=_params(("arbitrary",)),
        name="ada_mod",
    )(c_all, w_ada, b_ada.reshape(1, n))


def _prenorm_kernel(x_ref, ctx_ref, mod_ref, g_ref, o_ref, *, n_ctx_blk, d):
    j = pl.program_id(1)

    def emit(t):
        y = _rms(t, g_ref[...])
        o_ref[...] = (y * (1.0 + mod_ref[:, d:2 * d]) + mod_ref[:, 0:d]).astype(o_ref.dtype)

    @pl.when(j < n_ctx_blk)
    def _():
        emit(ctx_ref[...])

    @pl.when(j >= n_ctx_blk)
    def _():
        emit(x_ref[...])


def _row_block(b, blk, bsz, n_ctx_blk, n_x_blk):
    return jnp.where(blk < n_ctx_blk, b * n_ctx_blk + blk, bsz * n_ctx_blk + b * n_x_blk + blk - n_ctx_blk)


def _prenorm(x, ctx, mod3, g, rb):
    bsz, s, d = x.shape
    c = ctx.shape[1]
    n_ctx_blk, nb = c // rb, (c + s) // rb
    return pl.pallas_call(
        functools.partial(_prenorm_kernel, n_ctx_blk=n_ctx_blk, d=d),
        grid=(bsz, nb),
        in_specs=[
            pl.BlockSpec((None, rb, d), lambda b, j: (b, jnp.maximum(j - n_ctx_blk, 0), 0)),
            pl.BlockSpec((None, rb, d), lambda b, j: (b, jnp.minimum(j, n_ctx_blk - 1), 0)),
            pl.BlockSpec((None, 1, 2 * d), lambda b, j: (jnp.where(j < n_ctx_blk, bsz, b), 0, 0)),
            pl.BlockSpec((1, d), lambda b, j: (0, 0)),
        ],
        out_specs=pl.BlockSpec((rb, d), lambda b, j: (_row_block(b, j, bsz, n_ctx_blk, nb - n_ctx_blk), 0)),
        out_shape=jax.ShapeDtypeStruct((bsz * (c + s), d), BF16),
        compiler_params=_params(("arbitrary", "arbitrary")),
        name="prenorm_mod",
    )(x, ctx, mod3, g)


CAST_ROWS = 64


def _cast_specs(src, nsteps, lin):
    rows, cols = src.shape
    cr = CAST_ROWS
    while rows % cr or rows // cr > nsteps:
        cr += CAST_ROWS
    nblk = rows // cr
    spec = pl.BlockSpec((cr, cols), lambda *g: (jnp.minimum(lin(*g), nblk - 1), 0))
    return spec, spec, jax.ShapeDtypeStruct((rows, cols), BF16)


_NT = (((1,), (1,)), ((), ()))


def _qkproj_kernel(a_ref, w_ref, cos_ref, sin_ref, z_ref, *, dk):
    r = lax.dot_general(a_ref[...], w_ref[...], _NT, preferred_element_type=F32)
    cos, sin = cos_ref[...], sin_ref[...]
    for h in range(r.shape[1] // dk):
        cs = slice(h * dk, (h + 1) * dk)
        z_ref[:, cs] = (r[:, cs] * cos + pltpu.roll(r[:, cs], dk // 2, 1) * sin).astype(z_ref.dtype)


def _qkproj(a, w_t, cos_t, sin_t, bm, bn, *, dk, m_ctx, s_len):
    m, k = a.shape
    n = w_t.shape[0]
    n_ctx_tiles, tiles_per_batch = m_ctx // bm, s_len // bm
    assert m_ctx % bm == 0 and s_len % bm == 0

    def table_tile(i, j):
        return jnp.where(i < n_ctx_tiles, i, n_ctx_tiles + (i - n_ctx_tiles) % tiles_per_batch), 0

    return pl.pallas_call(
        functools.partial(_qkproj_kernel, dk=dk),
        grid=(m // bm, n // bn),
        in_specs=[pl.BlockSpec((bm, k), lambda i, j: (i, 0)),
                  pl.BlockSpec((bn, k), lambda i, j: (j, 0)),
                  pl.BlockSpec((bm, dk), table_tile),
                  pl.BlockSpec((bm, dk), table_tile)],
        out_specs=pl.BlockSpec((bm, bn), lambda i, j: (i, j)),
        out_shape=jax.ShapeDtypeStruct((m, n), BF16),
        compiler_params=_params(("arbitrary", "arbitrary")),
        name="in_proj_qk",
    )(a, w_t, cos_t, sin_t)


def _inproj_kernel(a_ref, w_ref, wlr_ref, src_ref, z_ref, lr_ref, dst_ref):
    a = a_ref[...]
    z_ref[...] = lax.dot_general(a, w_ref[...], _NT, preferred_element_type=F32).astype(z_ref.dtype)
    dst_ref[...] = src_ref[...].astype(dst_ref.dtype)

    @pl.when(pl.program_id(1) == 0)
    def _():
        lr_ref[...] = lax.dot_general(a, wlr_ref[...], _NT, preferred_element_type=F32)


def _inproj(a, w_t, row0, n, wlr_t, cast_src, bm, bn):
    m, k = a.shape
    ni, nj, j0 = m // bm, n // bn, row0 // bn
    assert row0 % bn == 0
    c_in, c_out, c_shape = _cast_specs(cast_src, ni * nj, lambda i, j: i * nj + j)
    return pl.pallas_call(
        _inproj_kernel,
        grid=(ni, nj),
        in_specs=[pl.BlockSpec((bm, k), lambda i, j: (i, 0)),
                  pl.BlockSpec((bn, k), lambda i, j: (j0 + j, 0)),
                  pl.BlockSpec((V7X_LANES, k), lambda i, j: (0, 0)),
                  c_in],
        out_specs=[pl.BlockSpec((bm, bn), lambda i, j: (i, j)),
                   pl.BlockSpec((bm, V7X_LANES), lambda i, j: (i, 0)),
                   c_out],
        out_shape=[jax.ShapeDtypeStruct((m, n), BF16),
                   jax.ShapeDtypeStruct((m, V7X_LANES), F32),
                   c_shape],
        compiler_params=_params(("arbitrary", "arbitrary")),
        name="in_proj_vr",
    )(a, w_t, wlr_t, cast_src)


def _matmul_kernel(a_ref, w_ref, *rest, nk, act, cast, w_t):
    if cast:
        src_ref, o_ref, dst_ref = rest[:3]
        dst_ref[...] = src_ref[...].astype(dst_ref.dtype)
    else:
        o_ref = rest[0]

    def product():
        if w_t:
            return lax.dot_general(a_ref[...], w_ref[...], _NT, preferred_element_type=F32)
        return jnp.dot(a_ref[...], w_ref[...], preferred_element_type=F32)

    def finish(r):
        if act == "relu2":
            r = jnp.square(jnp.maximum(r, 0.0))
        elif act == "gelu":
            r = _gelu(r)
        o_ref[...] = r.astype(o_ref.dtype)

    if nk == 1:
        finish(product())
        return
    acc_ref = rest[-1]
    kk = pl.program_id(2)

    @pl.when(kk == 0)
    def _():
        acc_ref[...] = product()

    @pl.when(jnp.logical_and(kk > 0, kk < nk - 1))
    def _():
        acc_ref[...] += product()

    @pl.when(kk == nk - 1)
    def _():
        finish(acc_ref[...] + product())


def _matmul(a, w, out_dtype, bm, bn, bk, act=None, cast_src=None, w_t=False, a_row0=0, name="matmul"):
    k = a.shape[1]
    m = a.shape[0] - a_row0
    n = w.shape[0] if w_t else w.shape[1]
    ni, nj, nk, i0 = m // bm, n // bn, k // bk, a_row0 // bm
    assert a_row0 % bm == 0
    w_spec = (pl.BlockSpec((bn, bk), lambda i, j, kk: (j, kk)) if w_t
              else pl.BlockSpec((bk, bn), lambda i, j, kk: (kk, j)))
    in_specs = [pl.BlockSpec((bm, bk), lambda i, j, kk: (i0 + i, kk)), w_spec]
    out_specs = [pl.BlockSpec((bm, bn), lambda i, j, kk: (i, j))]
    out_shape = [jax.ShapeDtypeStruct((m, n), out_dtype)]
    args = [a, w]
    if cast_src is not None:
        c_in, c_out, c_shape = _cast_specs(cast_src, ni * nj * nk, lambda i, j, kk: (i * nj + j) * nk + kk)
        in_specs.append(c_in)
        out_specs.append(c_out)
        out_shape.append(c_shape)
        args.append(cast_src)
    res = pl.pallas_call(
        functools.partial(_matmul_kernel, nk=nk, act=act, cast=cast_src is not None, w_t=w_t),
        grid=(ni, nj, nk),
        in_specs=in_specs,
        out_specs=out_specs,
        out_shape=out_shape,
        scratch_shapes=[pltpu.VMEM((bm, bn), F32)] if nk > 1 else [],
        compiler_params=_params(("arbitrary", "arbitrary", "arbitrary")),
        name=name,
    )(*args)
    return res if cast_src is not None else res[0]


_Dir = collections.namedtuple("_Dir", "q k v lr wd bd cmat o st reverse")


def _gla_block(dirs, emit, *, hb, dk, dv, tb):
    nch = tb // GLA_CHUNK
    nt = _NT
    tn = (((0,), (0,)), ((), ()))

    def ks(h):
        return slice(h * dk, (h + 1) * dk)

    def vs(h):
        return slice(h * dv, (h + 1) * dv)

    def rows(c):
        return slice(c * GLA_CHUNK, (c + 1) * GLA_CHUNK)

    gates = []
    for d in dirs:
        a = jnp.dot(d.lr[...].astype(BF16), d.wd[...].astype(BF16), preferred_element_type=F32) + d.bd[...]
        la = (jnp.minimum(a, 0.0) - jnp.log(1.0 + jnp.exp(-jnp.abs(a)))) * (LOG2_E / GLA_TAU)
        hi = la.astype(BF16)
        lo = (la - hi.astype(F32)).astype(BF16)
        tri = d.cmat[...]
        gates.append((jnp.dot(tri, hi, preferred_element_type=F32)
                      + jnp.dot(tri, lo, preferred_element_type=F32), tri))

    work = []
    for d, (cum, tri) in zip(dirs, gates):
        kr = d.k[...].astype(F32)
        ends = [c * GLA_CHUNK if d.reverse else (c + 1) * GLA_CHUNK - 1 for c in range(nch)]
        tot = [cum[e:e + 1] for e in ends]
        kd = [(kr[rows(c)] * jnp.exp2(tot[c] - cum[rows(c)])).astype(BF16) for c in range(nch)]
        qe = ke = None
        if emit:
            qe = (d.q[...].astype(F32) * (jnp.exp2(cum) * dk ** -0.5)).astype(BF16)
            ke = (kr * jnp.exp2(-cum)).astype(BF16)
        work.append((d, d.v[...], qe, ke, kd, tot, tri))

    heads = range(hb)
    intra = {}
    if emit:
        att = {(i, h): lax.dot_general(qe[:, ks(h)], ke[:, ks(h)], nt, preferred_element_type=F32)
               for i, (_, _, qe, ke, _, _, _) in enumerate(work) for h in heads}
        for i, (_, v, _, _, _, _, tri) in enumerate(work):
            for h in heads:
                att_m = jnp.where(tri > 0, att[i, h].astype(BF16), jnp.zeros((), BF16))
                intra[i, h] = jnp.dot(att_m, v[:, vs(h)], preferred_element_type=F32)

    upd = {(i, h, c): lax.dot_general(v[rows(c), vs(h)], kd[c][:, ks(h)], tn, preferred_element_type=F32)
           for i, (_, v, _, _, kd, _, _) in enumerate(work) for h in heads for c in range(nch)}

    seen = {}
    for i, (d, _, _, _, _, tot, _) in enumerate(work):
        order = range(nch - 1, -1, -1) if d.reverse else range(nch)
        for h in heads:
            st = d.st[h]
            for c in order:
                if emit:
                    seen[i, h, c] = st.astype(BF16)
                st = st * jnp.exp2(tot[c][:, ks(h)]) + upd[i, h, c]
            d.st[h] = st

    if emit:
        for i, (d, _, qe, _, _, _, _) in enumerate(work):
            for h in heads:
                for c in range(nch):
                    inter = lax.dot_general(qe[rows(c), ks(h)], seen[i, h, c], nt, preferred_element_type=F32)
                    d.o[rows(c), vs(h)] = (intra[i, h][rows(c)] + inter).astype(d.o.dtype)


def _gla_kernel(qf, kf, vf, lrf, qb, kb, vb, lrb,
                wdf, bdf, wdb, bdb, cmf, cmb, of_ref, ob_ref, stf, stb, *, n_ctx_blk, **kw):
    s = pl.program_id(2)
    dirs = (_Dir(qf, kf, vf, lrf, wdf, bdf, cmf, of_ref, stf, False),
            _Dir(qb, kb, vb, lrb, wdb, bdb, cmb, ob_ref, stb, True))

    @pl.when(s == 0)
    def _():
        stf[...] = jnp.zeros_like(stf)
        stb[...] = jnp.zeros_like(stb)

    @pl.when(s < n_ctx_blk)
    def _():
        _gla_block(dirs, False, **kw)

    @pl.when(s >= n_ctx_blk)
    def _():
        _gla_block(dirs, True, **kw)


def _gla(zqk, zvr, lr, wdf, bdf, wdb, bdb, cmf, cmb, *, bsz, s_len, c_len, tb, hb, dk, dv):
    n_ctx_blk, n_x_blk = c_len // tb, s_len // tb
    nb = n_ctx_blk + n_x_blk
    key_w = GLA_HEADS * dk
    qw, vw = hb * dk, hb * dv
    k_off = key_w // qw

    def fblk(s):
        return s

    def bblk(s):
        return jnp.where(s < n_ctx_blk, n_ctx_blk - 1 - s, nb - 1 - s + n_ctx_blk)

    def specs(blk):
        def row(b, s):
            return _row_block(b, blk(s), bsz, n_ctx_blk, n_x_blk)

        return [
            pl.BlockSpec((tb, qw), lambda b, g, s: (row(b, s), g)),
            pl.BlockSpec((tb, qw), lambda b, g, s: (row(b, s), k_off + g)),
            pl.BlockSpec((tb, vw), lambda b, g, s: (row(b, s), g)),
            pl.BlockSpec((tb, V7X_LANES), lambda b, g, s: (row(b, s), 0)),
        ]

    def wspec():
        return [pl.BlockSpec((V7X_LANES, qw), lambda b, g, s: (0, g)),
                pl.BlockSpec((1, qw), lambda b, g, s: (0, g))]

    def oblk_f(b, g, s):
        return (b * n_x_blk + jnp.maximum(s - n_ctx_blk, 0), g)

    def oblk_b(b, g, s):
        return (b * n_x_blk + jnp.minimum(nb - 1 - s + n_ctx_blk, nb - 1) - n_ctx_blk, g)

    cspec = pl.BlockSpec((tb, tb), lambda b, g, s: (0, 0))
    kern = functools.partial(_gla_kernel, n_ctx_blk=n_ctx_blk, hb=hb, dk=dk, dv=dv, tb=tb)
    o_shape = jax.ShapeDtypeStruct((bsz * s_len, GLA_HEADS * dv), BF16)
    return pl.pallas_call(
        kern,
        grid=(bsz, GLA_HEADS // hb, nb),
        in_specs=specs(fblk) + specs(bblk) + wspec() + wspec() + [cspec, cspec],
        out_specs=[pl.BlockSpec((tb, vw), oblk_f), pl.BlockSpec((tb, vw), oblk_b)],
        out_shape=[o_shape, o_shape],
        scratch_shapes=[pltpu.VMEM((hb, dv, dk), F32), pltpu.VMEM((hb, dv, dk), F32)],
        compiler_params=_params(("arbitrary", "arbitrary", "arbitrary")),
        name="gla_scan",
    )(zqk, zqk, zvr, lr, zqk, zqk, zvr, lr, wdf, bdf, wdb, bdb, cmf, cmb)


def _mix_kernel(of_ref, ob_ref, r_ref, u_ref, vv_ref, gn_ref, lg_ref, lb_ref, ws_ref, bs_ref, y_ref,
                *, dv, val_w, groups, tb):
    o = of_ref[...].astype(F32) + ob_ref[...].astype(F32)
    for h in range(val_w // dv):
        cs = slice(h * dv, (h + 1) * dv)
        r = r_ref[:, cs].astype(F32)
        y_ref[:, cs] = (_rms(o[:, cs], gn_ref[:, cs]) * _silu(r)).astype(y_ref.dtype)

    u = u_ref[...].astype(F32)
    vv = vv_ref[...].astype(F32)
    mu = jnp.mean(vv, axis=-1, keepdims=True)
    cen = vv - mu
    var = jnp.mean(cen * cen, axis=-1, keepdims=True)
    vn = (cen * lax.rsqrt(var + EPS) * lg_ref[...] + lb_ref[...]).astype(BF16)
    sgw = vn.shape[1]
    gw = sgw // groups
    for c in range(tb // SG_CHUNK):
        rows = slice(c * SG_CHUNK, (c + 1) * SG_CHUNK)
        for g in range(groups):
            cs = slice(g * gw, (g + 1) * gw)
            sg = jnp.dot(ws_ref[g], vn[rows, cs], preferred_element_type=F32) + bs_ref[:, cs]
            y_ref[rows, val_w + g * gw:val_w + (g + 1) * gw] = (u[rows, cs] * sg).astype(y_ref.dtype)


def _mix(o_f, o_b, z, zg, gn, lg, lb, ws, bsx, *, bsz, s_len, c_len, tb, dv, val_w, sgw, r_off):
    t = bsz * s_len
    groups = ws.shape[0]

    def zrow(i):
        return bsz * c_len // tb + i

    kern = functools.partial(_mix_kernel, dv=dv, val_w=val_w, groups=groups, tb=tb)
    return pl.pallas_call(
        kern,
        grid=(t // tb,),
        in_specs=[pl.BlockSpec((tb, val_w), lambda i: (i, 0)),
                  pl.BlockSpec((tb, val_w), lambda i: (i, 0)),
                  pl.BlockSpec((tb, val_w), lambda i: (zrow(i), r_off // val_w)),
                  pl.BlockSpec((tb, sgw), lambda i: (i, 0)),
                  pl.BlockSpec((tb, sgw), lambda i: (i, 1)),
                  pl.BlockSpec((1, val_w), lambda i: (0, 0)),
                  pl.BlockSpec((1, sgw), lambda i: (0, 0)),
                  pl.BlockSpec((1, sgw), lambda i: (0, 0)),
                  pl.BlockSpec((groups, SG_CHUNK, SG_CHUNK), lambda i: (0, 0, 0)),
                  pl.BlockSpec((SG_CHUNK, sgw), lambda i: (0, 0))],
        out_specs=pl.BlockSpec((tb, val_w + sgw), lambda i: (i, 0)),
        out_shape=jax.ShapeDtypeStruct((t, val_w + sgw), BF16),
        compiler_params=_params(("arbitrary",)),
        name="mix_readout",
    )(o_f, o_b, z, zg, zg, gn, lg, lb, ws, bsx)


def _res1_kernel(x_ref, m_ref, g1_ref, sh_ref, sc_ref, pg_ref, ng_ref, x1_ref, h2_ref):
    x1 = x_ref[...] + g1_ref[...] * _rms(m_ref[...].astype(F32), pg_ref[...])
    x1_ref[...] = x1
    h2_ref[...] = (_rms(x1, ng_ref[...]) * (1.0 + sc_ref[...]) + sh_ref[...]).astype(h2_ref.dtype)


def _mod_spec(n_x, d, col):
    return pl.BlockSpec((None, 1, d), lambda i: (i // n_x, 0, col))


def _res1(x2, mix, mod3, post_g, pre_g, *, s_len, tb):
    t, d = x2.shape
    n_x = s_len // tb
    row = pl.BlockSpec((tb, d), lambda i: (i, 0))
    vec = pl.BlockSpec((1, d), lambda i: (0, 0))
    return pl.pallas_call(
        _res1_kernel,
        grid=(t // tb,),
        in_specs=[row, row, _mod_spec(n_x, d, 2), _mod_spec(n_x, d, 3), _mod_spec(n_x, d, 4), vec, vec],
        out_specs=[row, row],
        out_shape=[jax.ShapeDtypeStruct((t, d), F32), jax.ShapeDtypeStruct((t, d), BF16)],
        compiler_params=_params(("arbitrary",)),
        name="residual1",
    )(x2, mix, mod3, mod3, mod3, post_g, pre_g)


def _res2_kernel(x_ref, m_ref, g2_ref, pg_ref, o_ref):
    o_ref[...] = x_ref[...] + g2_ref[...] * _rms(m_ref[...].astype(F32), pg_ref[...])


def _res2(x1, m2, mod3, post_g, *, s_len, tb):
    t, d = x1.shape
    n_x = s_len // tb
    row = pl.BlockSpec((tb, d), lambda i: (i, 0))
    return pl.pallas_call(
        _res2_kernel,
        grid=(t // tb,),
        in_specs=[row, row, _mod_spec(n_x, d, 5), pl.BlockSpec((1, d), lambda i: (0, 0))],
        out_specs=row,
        out_shape=jax.ShapeDtypeStruct((t, d), F32),
        compiler_params=_params(("arbitrary",)),
        name="residual2",
    )(x1, m2, mod3, post_g)


def _pair_order(t, axis, dk):
    shp = t.shape
    t = t.reshape(shp[:axis] + (shp[axis] // dk, 2, 2, dk // 4) + shp[axis + 1:])
    return jnp.swapaxes(t, axis + 1, axis + 2).reshape(shp)


def _rope_tables(s_len, m_ctx, dk):
    m = dk // 4
    inv_freq = ROPE_BASE ** (-jnp.arange(m, dtype=F32) / m)
    pos = jnp.arange(s_len)
    ang_r = (pos // GRID_W).astype(F32)[:, None] * inv_freq[None, :]
    ang_c = (pos % GRID_W).astype(F32)[:, None] * inv_freq[None, :]
    cos = jnp.concatenate([jnp.cos(ang_r), jnp.cos(ang_c)] * 2, axis=-1)
    sin = jnp.concatenate([-jnp.sin(ang_r), -jnp.sin(ang_c), jnp.sin(ang_r), jnp.sin(ang_c)], axis=-1)
    return (jnp.concatenate([jnp.ones((m_ctx, dk), F32), cos], axis=0),
            jnp.concatenate([jnp.zeros((m_ctx, dk), F32), sin], axis=0))


def _chunk_matrices(tb):
    i = jnp.arange(tb)
    same = (i[:, None] // GLA_CHUNK) == (i[None, :] // GLA_CHUNK)
    return ((same & (i[None, :] <= i[:, None])).astype(BF16),
            (same & (i[None, :] >= i[:, None])).astype(BF16))


def _tile(n, pref):
    return pref if n % pref == 0 else n


def kernel(x, c, ctx, c_ctx, w_ada, b_ada, pre1_g, post1_g, pre2_g, post2_g, w_in, w_dec_f, b_dec_f,
           w_dec_b, b_dec_b, gla_norm_g, sg_ln_g, sg_ln_b, w_s, b_s, w_o, w_1, w_2):
    bsz, s_len, d = x.shape
    c_len = ctx.shape[1]
    assert w_ada.shape[0] == 1, "single-layer block only"
    assert bsz < MOD_ROWS
    lowrank, key_w = w_dec_f.shape[1], w_dec_f.shape[2]
    dk = key_w // GLA_HEADS
    dv = gla_norm_g.shape[2]
    val_w = GLA_HEADS * dv
    sgw = sg_ln_g.shape[1]
    lf0 = 2 * key_w + 2 * val_w
    sg0 = lf0 + 2 * lowrank
    assert dk == V7X_LANES and 2 * lowrank <= V7X_LANES and w_in.shape[2] == sg0 + 2 * sgw
    t = bsz * s_len

    w_in_t = jnp.swapaxes(w_in[0], 0, 1).astype(BF16)
    w_sg_t = w_in_t[sg0:]
    w_qk_t = _pair_order(w_in_t[:2 * key_w], 0, dk)
    w_lr_t = jnp.pad(w_in_t[lf0:sg0], ((0, V7X_LANES - 2 * lowrank), (0, 0)))
    wdf = jnp.pad(_pair_order(w_dec_f[0], 1, dk), ((0, V7X_LANES - lowrank), (0, 0)))
    wdb = jnp.pad(_pair_order(w_dec_b[0], 1, dk), ((lowrank, V7X_LANES - 2 * lowrank), (0, 0)))
    bdf, bdb = _pair_order(b_dec_f, 1, dk), _pair_order(b_dec_b, 1, dk)

    c_all = jnp.concatenate([c, c_ctx[None], jnp.zeros((MOD_ROWS - bsz - 1, d), F32)], axis=0)
    mod = _ada(c_all, w_ada[0], b_ada[0])
    mod3 = mod.reshape(MOD_ROWS, 1, N_MOD * d)

    rb = _tile(math.gcd(s_len, c_len), ROW_BLOCK)
    mm = MATMUL_TILE
    x2 = x.reshape(t, d)
    hx = _prenorm(x, ctx, mod3, pre1_g, rb)

    m_ctx = bsz * c_len
    bm_all = _tile(m_ctx + t, mm)
    cos_t, sin_t = _rope_tables(s_len, m_ctx, dk)
    zqk = _qkproj(hx, w_qk_t, cos_t, sin_t, _tile(math.gcd(m_ctx, s_len), mm), _tile(2 * key_w, mm),
                  dk=dk, m_ctx=m_ctx, s_len=s_len)
    zvr, lr, w_o_b = _inproj(hx, w_in_t, 2 * key_w, 2 * val_w, w_lr_t, w_o[0], bm_all,
                             _tile(math.gcd(2 * key_w, 2 * val_w), mm))
    zg = _matmul(hx, w_sg_t, BF16, _tile(math.gcd(m_ctx, t), mm), _tile(2 * sgw, mm), d, act="gelu",
                 w_t=True, a_row0=m_ctx, name="in_proj_sg")

    cmf, cmb = _chunk_matrices(rb)
    o_f, o_b = _gla(zqk, zvr, lr, wdf, bdf, wdb, bdb, cmf, cmb,
                    bsz=bsz, s_len=s_len, c_len=c_len, tb=rb, hb=GLA_HEADS, dk=dk, dv=dv)

    bsx = jnp.repeat(b_s[0].T, sgw // w_s.shape[1], axis=1)
    y = _mix(o_f, o_b, zvr, zg, gla_norm_g.reshape(1, val_w), sg_ln_g, sg_ln_b, w_s[0].astype(BF16), bsx,
             bsz=bsz, s_len=s_len, c_len=c_len, tb=rb, dv=dv, val_w=val_w, sgw=sgw, r_off=val_w)

    bm = _tile(t, mm)
    mix, w_1_b = _matmul(y, w_o_b, BF16, bm, _tile(d, mm), d, cast_src=w_1[0], name="out_proj")
    x1, h2 = _res1(x2, mix, mod3, post1_g, pre2_g, s_len=s_len, tb=rb)
    d_ff = w_1_b.shape[1]
    hmid, w_2_b = _matmul(h2, w_1_b, BF16, bm, _tile(d_ff, mm), d, act="relu2", cast_src=w_2[0],
                          name="mlp_up")
    m2 = _matmul(hmid, w_2_b, BF16, bm, _tile(d, mm), _tile(d_ff, MLP_DOWN_K_TILE), name="mlp_down")
    out = _res2(x1, m2, mod3, post2_g, s_len=s_len, tb=rb)
    return out.reshape(bsz, s_len, d)
```

```python
import collections
import functools
import math

import jax
import jax.numpy as jnp
from jax import lax
from jax.experimental import pallas as pl
from jax.experimental.pallas import tpu as pltpu

GRID_W = 64
GLA_HEADS = 8
GLA_TAU = 16.0
ROPE_BASE = 10000.0
N_MOD = 6
EPS = 1e-6
SG_CHUNK = 128
GLA_CHUNK = 64
LOG2_E = math.log2(math.e)

V7X_LANES = 128
MOD_ROWS = 8
VMEM_LIMIT_BYTES = 56 * 1024 * 1024
MATMUL_TILE = 1024
MLP_DOWN_K_TILE = 4096
ROW_BLOCK = 256
LATENT_ROW_BLOCK = 512

F32 = jnp.float32
BF16 = jnp.bfloat16


def _params(sem, vmem=VMEM_LIMIT_BYTES):
    return pltpu.CompilerParams(dimension_semantics=sem, vmem_limit_bytes=vmem)


def _rms(t, g):
    return t * lax.rsqrt(jnp.mean(t * t, axis=-1, keepdims=True) + EPS) * g


def _silu(t):
    return t * (1.0 / (1.0 + jnp.exp(-t)))


def _gelu(t):
    return 0.5 * t * (1.0 + lax.erf(t * (2.0 ** -0.5)))


def _ada_kernel(c_ref, w_ref, b_ref, o_ref):
    cond = _silu(c_ref[...])
    o_ref[...] = jnp.dot(cond.astype(BF16), w_ref[...].astype(BF16),
                         preferred_element_type=F32) + b_ref[...]


def _ada(c_all, w_ada, b_ada):
    d, n = w_ada.shape
    bn = _tile(n, MATMUL_TILE)
    return pl.pallas_call(
        _ada_kernel,
        grid=(n // bn,),
        in_specs=[pl.BlockSpec((MOD_ROWS, d), lambda j: (0, 0)),
                  pl.BlockSpec((d, bn), lambda j: (0, j)),
                  pl.BlockSpec((1, bn), lambda j: (0, j))],
        out_specs=pl.BlockSpec((MOD_ROWS, bn), lambda j: (0, j)),
        out_shape=jax.ShapeDtypeStruct((MOD_ROWS, n), F32),
        compiler_params=_params(("arbitrary",)),
        name="ada_mod",
    )(c_all, w_ada, b_ada.reshape(1, n))


def _prenorm_kernel(x_ref, ctx_ref, mod_ref, g_ref, o_ref, *, n_ctx_blk, d):
    j = pl.program_id(1)

    def emit(t):
        y = _rms(t, g_ref[...])
        o_ref[...] = (y * (1.0 + mod_ref[:, d:2 * d]) + mod_ref[:, 0:d]).astype(o_ref.dtype)

    @pl.when(j < n_ctx_blk)
    def _():
        emit(ctx_ref[...])

    @pl.when(j >= n_ctx_blk)
    def _():
        emit(x_ref[...])


def _row_block(b, blk, bsz, n_ctx_blk, n_x_blk):
    return jnp.where(blk < n_ctx_blk, b * n_ctx_blk + blk, bsz * n_ctx_blk + b * n_x_blk + blk - n_ctx_blk)


def _prenorm(x, ctx, mod3, g, rb):
    bsz, s, d = x.shape
    c = ctx.shape[1]
    n_ctx_blk, nb = c // rb, (c + s) // rb
    return pl.pallas_call(
        functools.partial(_prenorm_kernel, n_ctx_blk=n_ctx_blk, d=d),
        grid=(bsz, nb),
        in_specs=[
            pl.BlockSpec((None, rb, d), lambda b, j: (b, jnp.maximum(j - n_ctx_blk, 0), 0)),
            pl.BlockSpec((None, rb, d), lambda b, j: (b, jnp.minimum(j, n_ctx_blk - 1), 0)),
            pl.BlockSpec((None, 1, 2 * d), lambda b, j: (jnp.where(j < n_ctx_blk, bsz, b), 0, 0)),
            pl.BlockSpec((1, d), lambda b, j: (0, 0)),
        ],
        out_specs=pl.BlockSpec((rb, d), lambda b, j: (_row_block(b, j, bsz, n_ctx_blk, nb - n_ctx_blk), 0)),
        out_shape=jax.ShapeDtypeStruct((bsz * (c + s), d), BF16),
        compiler_params=_params(("arbitrary", "arbitrary")),
        name="prenorm_mod",
    )(x, ctx, mod3, g)


CAST_ROWS = 64


def _cast_specs(src, nsteps, lin):
    rows, cols = src.shape
    cr = CAST_ROWS
    while rows % cr or rows // cr > nsteps:
        cr += CAST_ROWS
    nblk = rows // cr
    spec = pl.BlockSpec((cr, cols), lambda *g: (jnp.minimum(lin(*g), nblk - 1), 0))
    return spec, spec, jax.ShapeDtypeStruct((rows, cols), BF16)


_NT = (((1,), (1,)), ((), ()))


def _qkproj_kernel(a_ref, w_ref, cos_ref, sin_ref, z_ref, *, dk):
    r = lax.dot_general(a_ref[...], w_ref[...], _NT, preferred_element_type=F32)
    cos, sin = cos_ref[...], sin_ref[...]
    for h in range(r.shape[1] // dk):
        cs = slice(h * dk, (h + 1) * dk)
        z_ref[:, cs] = (r[:, cs] * cos + pltpu.roll(r[:, cs], dk // 2, 1) * sin).astype(z_ref.dtype)


def _qkproj(a, w_t, cos_t, sin_t, bm, bn, *, dk, m_ctx, s_len):
    m, k = a.shape
    n = w_t.shape[0]
    n_ctx_tiles, tiles_per_batch = m_ctx // bm, s_len // bm
    assert m_ctx % bm == 0 and s_len % bm == 0

    def table_tile(i, j):
        return jnp.where(i < n_ctx_tiles, i, n_ctx_tiles + (i - n_ctx_tiles) % tiles_per_batch), 0

    return pl.pallas_call(
        functools.partial(_qkproj_kernel, dk=dk),
        grid=(m // bm, n // bn),
        in_specs=[pl.BlockSpec((bm, k), lambda i, j: (i, 0)),
                  pl.BlockSpec((bn, k), lambda i, j: (j, 0)),
                  pl.BlockSpec((bm, dk), table_tile),
                  pl.BlockSpec((bm, dk), table_tile)],
        out_specs=pl.BlockSpec((bm, bn), lambda i, j: (i, j)),
        out_shape=jax.ShapeDtypeStruct((m, n), BF16),
        compiler_params=_params(("arbitrary", "arbitrary")),
        name="in_proj_qk",
    )(a, w_t, cos_t, sin_t)


def _inproj_kernel(a_ref, w_ref, wlr_ref, src_ref, z_ref, lr_ref, dst_ref):
    a = a_ref[...]
    z_ref[...] = lax.dot_general(a, w_ref[...], _NT, preferred_element_type=F32).astype(z_ref.dtype)
    dst_ref[...] = src_ref[...].astype(dst_ref.dtype)

    @pl.when(pl.program_id(1) == 0)
    def _():
        lr_ref[...] = lax.dot_general(a, wlr_ref[...], _NT, preferred_element_type=F32)


def _inproj(a, w_t, row0, n, wlr_t, cast_src, bm, bn):
    m, k = a.shape
    ni, nj, j0 = m // bm, n // bn, row0 // bn
    assert row0 % bn == 0
    c_in, c_out, c_shape = _cast_specs(cast_src, ni * nj, lambda i, j: i * nj + j)
    return pl.pallas_call(
        _inproj_kernel,
        grid=(ni, nj),
        in_specs=[pl.BlockSpec((bm, k), lambda i, j: (i, 0)),
                  pl.BlockSpec((bn, k), lambda i, j: (j0 + j, 0)),
                  pl.BlockSpec((V7X_LANES, k), lambda i, j: (0, 0)),
                  c_in],
        out_specs=[pl.BlockSpec((bm, bn), lambda i, j: (i, j)),
                   pl.BlockSpec((bm, V7X_LANES), lambda i, j: (i, 0)),
                   c_out],
        out_shape=[jax.ShapeDtypeStruct((m, n), BF16),
                   jax.ShapeDtypeStruct((m, V7X_LANES), F32),
                   c_shape],
        compiler_params=_params(("arbitrary", "arbitrary")),
        name="in_proj_vr",
    )(a, w_t, wlr_t, cast_src)


def _matmul_kernel(a_ref, w_ref, *rest, nk, act, cast, w_t):
    if cast:
        src_ref, o_ref, dst_ref = rest[:3]
        dst_ref[...] = src_ref[...].astype(dst_ref.dtype)
    else:
        o_ref = rest[0]

    def product():
        if w_t:
            return lax.dot_general(a_ref[...], w_ref[...], _NT, preferred_element_type=F32)
        return jnp.dot(a_ref[...], w_ref[...], preferred_element_type=F32)

    def finish(r):
        if act == "relu2":
            r = jnp.square(jnp.maximum(r, 0.0))
        elif act == "gelu":
            r = _gelu(r)
        o_ref[...] = r.astype(o_ref.dtype)

    if nk == 1:
        finish(product())
        return
    acc_ref = rest[-1]
    kk = pl.program_id(2)

    @pl.when(kk == 0)
    def _():
        acc_ref[...] = product()

    @pl.when(jnp.logical_and(kk > 0, kk < nk - 1))
    def _():
        acc_ref[...] += product()

    @pl.when(kk == nk - 1)
    def _():
        finish(acc_ref[...] + product())


def _matmul(a, w, out_dtype, bm, bn, bk, act=None, cast_src=None, w_t=False, a_row0=0, w_row0=0, n=None,
            name="matmul"):
    k = a.shape[1]
    m = a.shape[0] - a_row0
    if n is None:
        n = w.shape[0] if w_t else w.shape[1]
    ni, nj, nk, i0 = m // bm, n // bn, k // bk, a_row0 // bm
    assert a_row0 % bm == 0 and (w_t or w_row0 == 0)
    if not w_t:
        w_spec = pl.BlockSpec((bk, bn), lambda i, j, kk: (kk, j))
    elif w_row0 % bn == 0:
        w_spec = pl.BlockSpec((bn, bk), lambda i, j, kk: (w_row0 // bn + j, kk))
    else:
        assert w_row0 % 16 == 0 and bn % 16 == 0, "bf16 rows come in sublane pairs of 8"
        w_spec = pl.BlockSpec((pl.Element(bn), pl.Element(bk)),
                              lambda i, j, kk: (pl.multiple_of(w_row0 + j * bn, 16), pl.multiple_of(kk * bk, 128)))
    in_specs = [pl.BlockSpec((bm, bk), lambda i, j, kk: (i0 + i, kk)), w_spec]
    out_specs = [pl.BlockSpec((bm, bn), lambda i, j, kk: (i, j))]
    out_shape = [jax.ShapeDtypeStruct((m, n), out_dtype)]
    args = [a, w]
    if cast_src is not None:
        c_in, c_out, c_shape = _cast_specs(cast_src, ni * nj * nk, lambda i, j, kk: (i * nj + j) * nk + kk)
        in_specs.append(c_in)
        out_specs.append(c_out)
        out_shape.append(c_shape)
        args.append(cast_src)
    res = pl.pallas_call(
        functools.partial(_matmul_kernel, nk=nk, act=act, cast=cast_src is not None, w_t=w_t),
        grid=(ni, nj, nk),
        in_specs=in_specs,
        out_specs=out_specs,
        out_shape=out_shape,
        scratch_shapes=[pltpu.VMEM((bm, bn), F32)] if nk > 1 else [],
        compiler_params=_params(("arbitrary", "arbitrary", "arbitrary")),
        name=name,
    )(*args)
    return res if cast_src is not None else res[0]


_Dir = collections.namedtuple("_Dir", "q k v lr wd bd cmat o st reverse")


def _gla_block(dirs, emit, *, hb, dk, dv, tb):
    nch = tb // GLA_CHUNK
    nt = _NT
    tn = (((0,), (0,)), ((), ()))

    def ks(h):
        return slice(h * dk, (h + 1) * dk)

    def vs(h):
        return slice(h * dv, (h + 1) * dv)

    def rows(c):
        return slice(c * GLA_CHUNK, (c + 1) * GLA_CHUNK)

    gates = []
    for d in dirs:
        a = jnp.dot(d.lr[...].astype(BF16), d.wd[...].astype(BF16), preferred_element_type=F32) + d.bd[...]
        la = (jnp.minimum(a, 0.0) - jnp.log(1.0 + jnp.exp(-jnp.abs(a)))) * (LOG2_E / GLA_TAU)
        hi = la.astype(BF16)
        lo = (la - hi.astype(F32)).astype(BF16)
        tri = d.cmat[...]
        gates.append((jnp.dot(tri, hi, preferred_element_type=F32)
                      + jnp.dot(tri, lo, preferred_element_type=F32), tri))

    work = []
    for d, (cum, tri) in zip(dirs, gates):
        kr = d.k[...].astype(F32)
        ends = [c * GLA_CHUNK if d.reverse else (c + 1) * GLA_CHUNK - 1 for c in range(nch)]
        tot = [cum[e:e + 1] for e in ends]
        kd = [(kr[rows(c)] * jnp.exp2(tot[c] - cum[rows(c)])).astype(BF16) for c in range(nch)]
        qe = ke = None
        if emit:
            qe = (d.q[...].astype(F32) * (jnp.exp2(cum) * dk ** -0.5)).astype(BF16)
            ke = (kr * jnp.exp2(-cum)).astype(BF16)
        work.append((d, d.v[...], qe, ke, kd, tot, tri))

    heads = range(hb)
    intra = {}
    if emit:
        att = {(i, h): lax.dot_general(qe[:, ks(h)], ke[:, ks(h)], nt, preferred_element_type=F32)
               for i, (_, _, qe, ke, _, _, _) in enumerate(work) for h in heads}
        for i, (_, v, _, _, _, _, tri) in enumerate(work):
            for h in heads:
                att_m = jnp.where(tri > 0, att[i, h].astype(BF16), jnp.zeros((), BF16))
                intra[i, h] = jnp.dot(att_m, v[:, vs(h)], preferred_element_type=F32)

    upd = {(i, h, c): lax.dot_general(v[rows(c), vs(h)], kd[c][:, ks(h)], tn, preferred_element_type=F32)
           for i, (_, v, _, _, kd, _, _) in enumerate(work) for h in heads for c in range(nch)}

    seen = {}
    for i, (d, _, _, _, _, tot, _) in enumerate(work):
        order = range(nch - 1, -1, -1) if d.reverse else range(nch)
        for h in heads:
            st = d.st[h]
            for c in order:
                if emit:
                    seen[i, h, c] = st.astype(BF16)
                st = st * jnp.exp2(tot[c][:, ks(h)]) + upd[i, h, c]
            d.st[h] = st

    if emit:
        for i, (d, _, qe, _, _, _, _) in enumerate(work):
            for h in heads:
                for c in range(nch):
                    inter = lax.dot_general(qe[rows(c), ks(h)], seen[i, h, c], nt, preferred_element_type=F32)
                    d.o[rows(c), vs(h)] = (intra[i, h][rows(c)] + inter).astype(d.o.dtype)


def _gla_kernel(qf, kf, vf, lrf, qb, kb, vb, lrb,
                wdf, bdf, wdb, bdb, cmf, cmb, of_ref, ob_ref, stf, stb, *, n_ctx_blk, **kw):
    s = pl.program_id(2)
    dirs = (_Dir(qf, kf, vf, lrf, wdf, bdf, cmf, of_ref, stf, False),
            _Dir(qb, kb, vb, lrb, wdb, bdb, cmb, ob_ref, stb, True))

    @pl.when(s == 0)
    def _():
        stf[...] = jnp.zeros_like(stf)
        stb[...] = jnp.zeros_like(stb)

    @pl.when(s < n_ctx_blk)
    def _():
        _gla_block(dirs, False, **kw)

    @pl.when(s >= n_ctx_blk)
    def _():
        _gla_block(dirs, True, **kw)


def _gla(zqk, zvr, lr, wdf, bdf, wdb, bdb, cmf, cmb, *, bsz, s_len, c_len, tb, hb, dk, dv):
    n_ctx_blk, n_x_blk = c_len // tb, s_len // tb
    nb = n_ctx_blk + n_x_blk
    key_w = GLA_HEADS * dk
    qw, vw = hb * dk, hb * dv
    k_off = key_w // qw

    def fblk(s):
        return s

    def bblk(s):
        return jnp.where(s < n_ctx_blk, n_ctx_blk - 1 - s, nb - 1 - s + n_ctx_blk)

    def specs(blk):
        def row(b, s):
            return _row_block(b, blk(s), bsz, n_ctx_blk, n_x_blk)

        return [
            pl.BlockSpec((tb, qw), lambda b, g, s: (row(b, s), g)),
            pl.BlockSpec((tb, qw), lambda b, g, s: (row(b, s), k_off + g)),
            pl.BlockSpec((tb, vw), lambda b, g, s: (row(b, s), g)),
            pl.BlockSpec((tb, V7X_LANES), lambda b, g, s: (row(b, s), 0)),
        ]

    def wspec():
        return [pl.BlockSpec((V7X_LANES, qw), lambda b, g, s: (0, g)),
                pl.BlockSpec((1, qw), lambda b, g, s: (0, g))]

    def oblk_f(b, g, s):
        return (b * n_x_blk + jnp.maximum(s - n_ctx_blk, 0), g)

    def oblk_b(b, g, s):
        return (b * n_x_blk + jnp.minimum(nb - 1 - s + n_ctx_blk, nb - 1) - n_ctx_blk, g)

    cspec = pl.BlockSpec((tb, tb), lambda b, g, s: (0, 0))
    kern = functools.partial(_gla_kernel, n_ctx_blk=n_ctx_blk, hb=hb, dk=dk, dv=dv, tb=tb)
    o_shape = jax.ShapeDtypeStruct((bsz * s_len, GLA_HEADS * dv), BF16)
    return pl.pallas_call(
        kern,
        grid=(bsz, GLA_HEADS // hb, nb),
        in_specs=specs(fblk) + specs(bblk) + wspec() + wspec() + [cspec, cspec],
        out_specs=[pl.BlockSpec((tb, vw), oblk_f), pl.BlockSpec((tb, vw), oblk_b)],
        out_shape=[o_shape, o_shape],
        scratch_shapes=[pltpu.VMEM((hb, dv, dk), F32), pltpu.VMEM((hb, dv, dk), F32)],
        compiler_params=_params(("arbitrary", "arbitrary", "arbitrary")),
        name="gla_scan",
    )(zqk, zqk, zvr, lr, zqk, zqk, zvr, lr, wdf, bdf, wdb, bdb, cmf, cmb)


def _mix_kernel(of_ref, ob_ref, r_ref, u_ref, vv_ref, gn_ref, lg_ref, lb_ref, ws_ref, bs_ref, y_ref,
                *, dv, val_w, groups, tb):
    o = of_ref[...].astype(F32) + ob_ref[...].astype(F32)
    for h in range(val_w // dv):
        cs = slice(h * dv, (h + 1) * dv)
        r = r_ref[:, cs].astype(F32)
        y_ref[:, cs] = (_rms(o[:, cs], gn_ref[:, cs]) * _silu(r)).astype(y_ref.dtype)

    u = u_ref[...].astype(F32)
    vv = vv_ref[...].astype(F32)
    mu = jnp.mean(vv, axis=-1, keepdims=True)
    cen = vv - mu
    var = jnp.mean(cen * cen, axis=-1, keepdims=True)
    vn = (cen * lax.rsqrt(var + EPS) * lg_ref[...] + lb_ref[...]).astype(BF16)
    sgw = vn.shape[1]
    gw = sgw // groups
    for c in range(tb // SG_CHUNK):
        rows = slice(c * SG_CHUNK, (c + 1) * SG_CHUNK)
        for g in range(groups):
            cs = slice(g * gw, (g + 1) * gw)
            sg = jnp.dot(ws_ref[g], vn[rows, cs], preferred_element_type=F32) + bs_ref[:, cs]
            y_ref[rows, val_w + g * gw:val_w + (g + 1) * gw] = (u[rows, cs] * sg).astype(y_ref.dtype)


def _mix(o_f, o_b, z, zg, gn, lg, lb, ws, bsx, *, bsz, s_len, c_len, tb, dv, val_w, sgw, r_off):
    t = bsz * s_len
    groups = ws.shape[0]

    def zrow(i):
        return bsz * c_len // tb + i

    kern = functools.partial(_mix_kernel, dv=dv, val_w=val_w, groups=groups, tb=tb)
    return pl.pallas_call(
        kern,
        grid=(t // tb,),
        in_specs=[pl.BlockSpec((tb, val_w), lambda i: (i, 0)),
                  pl.BlockSpec((tb, val_w), lambda i: (i, 0)),
                  pl.BlockSpec((tb, val_w), lambda i: (zrow(i), r_off // val_w)),
                  pl.BlockSpec((tb, sgw), lambda i: (i, 0)),
                  pl.BlockSpec((tb, sgw), lambda i: (i, 1)),
                  pl.BlockSpec((1, val_w), lambda i: (0, 0)),
                  pl.BlockSpec((1, sgw), lambda i: (0, 0)),
                  pl.BlockSpec((1, sgw), lambda i: (0, 0)),
                  pl.BlockSpec((groups, SG_CHUNK, SG_CHUNK), lambda i: (0, 0, 0)),
                  pl.BlockSpec((SG_CHUNK, sgw), lambda i: (0, 0))],
        out_specs=pl.BlockSpec((tb, val_w + sgw), lambda i: (i, 0)),
        out_shape=jax.ShapeDtypeStruct((t, val_w + sgw), BF16),
        compiler_params=_params(("arbitrary",)),
        name="mix_readout",
    )(o_f, o_b, z, zg, zg, gn, lg, lb, ws, bsx)


def _res1_kernel(x_ref, m_ref, g1_ref, sh_ref, sc_ref, pg_ref, ng_ref, x1_ref, h2_ref):
    x1 = x_ref[...] + g1_ref[...] * _rms(m_ref[...].astype(F32), pg_ref[...])
    x1_ref[...] = x1
    h2_ref[...] = (_rms(x1, ng_ref[...]) * (1.0 + sc_ref[...]) + sh_ref[...]).astype(h2_ref.dtype)


def _mod_spec(n_x, d, col):
    return pl.BlockSpec((None, 1, d), lambda i: (i // n_x, 0, col))


def _res1(x2, mix, mod3, post_g, pre_g, *, s_len, tb):
    t, d = x2.shape
    n_x = s_len // tb
    row = pl.BlockSpec((tb, d), lambda i: (i, 0))
    vec = pl.BlockSpec((1, d), lambda i: (0, 0))
    return pl.pallas_call(
        _res1_kernel,
        grid=(t // tb,),
        in_specs=[row, row, _mod_spec(n_x, d, 2), _mod_spec(n_x, d, 3), _mod_spec(n_x, d, 4), vec, vec],
        out_specs=[row, row],
        out_shape=[jax.ShapeDtypeStruct((t, d), F32), jax.ShapeDtypeStruct((t, d), BF16)],
        compiler_params=_params(("arbitrary",)),
        name="residual1",
    )(x2, mix, mod3, mod3, mod3, post_g, pre_g)


def _res2_kernel(x_ref, m_ref, g2_ref, pg_ref, o_ref):
    o_ref[...] = x_ref[...] + g2_ref[...] * _rms(m_ref[...].astype(F32), pg_ref[...])


def _res2(x1, m2, mod3, post_g, *, s_len, tb):
    t, d = x1.shape
    n_x = s_len // tb
    row = pl.BlockSpec((tb, d), lambda i: (i, 0))
    return pl.pallas_call(
        _res2_kernel,
        grid=(t // tb,),
        in_specs=[row, row, _mod_spec(n_x, d, 5), pl.BlockSpec((1, d), lambda i: (0, 0))],
        out_specs=row,
        out_shape=jax.ShapeDtypeStruct((t, d), F32),
        compiler_params=_params(("arbitrary",)),
        name="residual2",
    )(x1, m2, mod3, post_g)


def _pair_order(t, axis, dk):
    shp = t.shape
    t = t.reshape(shp[:axis] + (shp[axis] // dk, 2, 2, dk // 4) + shp[axis + 1:])
    return jnp.swapaxes(t, axis + 1, axis + 2).reshape(shp)


def _rope_tables(s_len, m_ctx, dk):
    m = dk // 4
    inv_freq = ROPE_BASE ** (-jnp.arange(m, dtype=F32) / m)
    pos = jnp.arange(s_len)
    ang_r = (pos // GRID_W).astype(F32)[:, None] * inv_freq[None, :]
    ang_c = (pos % GRID_W).astype(F32)[:, None] * inv_freq[None, :]
    cos = jnp.concatenate([jnp.cos(ang_r), jnp.cos(ang_c)] * 2, axis=-1)
    sin = jnp.concatenate([-jnp.sin(ang_r), -jnp.sin(ang_c), jnp.sin(ang_r), jnp.sin(ang_c)], axis=-1)
    return (jnp.concatenate([jnp.ones((m_ctx, dk), F32), cos], axis=0),
            jnp.concatenate([jnp.zeros((m_ctx, dk), F32), sin], axis=0))


def _chunk_matrices(tb):
    i = jnp.arange(tb)
    same = (i[:, None] // GLA_CHUNK) == (i[None, :] // GLA_CHUNK)
    return ((same & (i[None, :] <= i[:, None])).astype(BF16),
            (same & (i[None, :] >= i[:, None])).astype(BF16))


def _tile(n, pref):
    return pref if n % pref == 0 else n


def kernel(x, c, ctx, c_ctx, w_ada, b_ada, pre1_g, post1_g, pre2_g, post2_g, w_in, w_dec_f, b_dec_f,
           w_dec_b, b_dec_b, gla_norm_g, sg_ln_g, sg_ln_b, w_s, b_s, w_o, w_1, w_2):
    bsz, s_len, d = x.shape
    c_len = ctx.shape[1]
    assert w_ada.shape[0] == 1, "single-layer block only"
    assert bsz < MOD_ROWS
    lowrank, key_w = w_dec_f.shape[1], w_dec_f.shape[2]
    dk = key_w // GLA_HEADS
    dv = gla_norm_g.shape[2]
    val_w = GLA_HEADS * dv
    sgw = sg_ln_g.shape[1]
    lf0 = 2 * key_w + 2 * val_w
    sg0 = lf0 + 2 * lowrank
    assert dk == V7X_LANES and 2 * lowrank <= V7X_LANES and w_in.shape[2] == sg0 + 2 * sgw
    t = bsz * s_len

    w_in_t = jnp.swapaxes(w_in[0], 0, 1).astype(BF16)
    w_qk_t = _pair_order(w_in_t[:2 * key_w], 0, dk)
    w_lr_t = jnp.pad(w_in_t[lf0:sg0], ((0, V7X_LANES - 2 * lowrank), (0, 0)))
    wdf = jnp.pad(_pair_order(w_dec_f[0], 1, dk), ((0, V7X_LANES - lowrank), (0, 0)))
    wdb = jnp.pad(_pair_order(w_dec_b[0], 1, dk), ((lowrank, V7X_LANES - 2 * lowrank), (0, 0)))
    bdf, bdb = _pair_order(b_dec_f, 1, dk), _pair_order(b_dec_b, 1, dk)

    c_all = jnp.concatenate([c, c_ctx[None], jnp.zeros((MOD_ROWS - bsz - 1, d), F32)], axis=0)
    mod = _ada(c_all, w_ada[0], b_ada[0])
    mod3 = mod.reshape(MOD_ROWS, 1, N_MOD * d)

    rb = _tile(math.gcd(s_len, c_len), ROW_BLOCK)
    mm = MATMUL_TILE
    x2 = x.reshape(t, d)
    hx = _prenorm(x, ctx, mod3, pre1_g, rb)

    m_ctx = bsz * c_len
    bm_all = _tile(m_ctx + t, mm)
    cos_t, sin_t = _rope_tables(s_len, m_ctx, dk)
    zqk = _qkproj(hx, w_qk_t, cos_t, sin_t, _tile(math.gcd(m_ctx, s_len), mm), _tile(2 * key_w, mm),
                  dk=dk, m_ctx=m_ctx, s_len=s_len)
    zvr, lr, w_o_b = _inproj(hx, w_in_t, 2 * key_w, 2 * val_w, w_lr_t, w_o[0], bm_all,
                             _tile(math.gcd(2 * key_w, 2 * val_w), mm))
    zg = _matmul(hx, w_in_t, BF16, _tile(math.gcd(m_ctx, t), mm), _tile(2 * sgw, mm), d, act="gelu",
                 w_t=True, a_row0=m_ctx, w_row0=sg0, n=2 * sgw, name="in_proj_sg")

    cmf, cmb = _chunk_matrices(rb)
    o_f, o_b = _gla(zqk, zvr, lr, wdf, bdf, wdb, bdb, cmf, cmb,
                    bsz=bsz, s_len=s_len, c_len=c_len, tb=rb, hb=GLA_HEADS, dk=dk, dv=dv)

    rl = _tile(math.gcd(m_ctx, s_len), LATENT_ROW_BLOCK)
    bsx = jnp.repeat(b_s[0].T, sgw // w_s.shape[1], axis=1)
    y = _mix(o_f, o_b, zvr, zg, gla_norm_g.reshape(1, val_w), sg_ln_g, sg_ln_b, w_s[0].astype(BF16), bsx,
             bsz=bsz, s_len=s_len, c_len=c_len, tb=rl, dv=dv, val_w=val_w, sgw=sgw, r_off=val_w)

    bm = _tile(t, mm)
    mix, w_1_b = _matmul(y, w_o_b, BF16, bm, _tile(d, mm), d, cast_src=w_1[0], name="out_proj")
    x1, h2 = _res1(x2, mix, mod3, post1_g, pre2_g, s_len=s_len, tb=rb)
    d_ff = w_1_b.shape[1]
    hmid, w_2_b = _matmul(h2, w_1_b, BF16, bm, _tile(d_ff, mm), d, act="relu2", cast_src=w_2[0],
                          name="mlp_up")
    m2 = _matmul(hmid, w_2_b, BF16, bm, _tile(d, mm), _tile(d_ff, MLP_DOWN_K_TILE), name="mlp_down")
    out = _res2(x1, m2, mod3, post2_g, s_len=s_len, tb=rl)
    return out.reshape(bsz, s_len, d)
```

```python
import collections
import functools
import math

import jax
import jax.numpy as jnp
from jax import lax
from jax.experimental import pallas as pl
from jax.experimental.pallas import tpu as pltpu

GRID_W = 64
GLA_HEADS = 8
GLA_TAU = 16.0
ROPE_BASE = 10000.0
N_MOD = 6
EPS = 1e-6
SG_CHUNK = 128
GLA_CHUNK = 64
LOG2_E = math.log2(math.e)

V7X_LANES = 128
MOD_ROWS = 8
VMEM_LIMIT_BYTES = 56 * 1024 * 1024
MATMUL_TILE = 1024
MLP_DOWN_K_TILE = 4096
ROW_BLOCK = 256
LATENT_ROW_BLOCK = 512
PRENORM_CHUNK = 128

F32 = jnp.float32
BF16 = jnp.bfloat16


def _params(sem, vmem=VMEM_LIMIT_BYTES):
    return pltpu.CompilerParams(dimension_semantics=sem, vmem_limit_bytes=vmem)


def _rms(t, g):
    return t * lax.rsqrt(jnp.mean(t * t, axis=-1, keepdims=True) + EPS) * g


def _silu(t):
    return t * (1.0 / (1.0 + jnp.exp(-t)))


def _gelu(t):
    return 0.5 * t * (1.0 + lax.erf(t * (2.0 ** -0.5)))


def _ada_kernel(c_ref, w_ref, b_ref, o_ref):
    cond = _silu(c_ref[...])
    o_ref[...] = jnp.dot(cond.astype(BF16), w_ref[...].astype(BF16),
                         preferred_element_type=F32) + b_ref[...]


def _ada(c_all, w_ada, b_ada):
    d, n = w_ada.shape
    bn = _tile(n, MATMUL_TILE)
    return pl.pallas_call(
        _ada_kernel,
        grid=(n // bn,),
        in_specs=[pl.BlockSpec((MOD_ROWS, d), lambda j: (0, 0)),
                  pl.BlockSpec((d, bn), lambda j: (0, j)),
                  pl.BlockSpec((1, bn), lambda j: (0, j))],
        out_specs=pl.BlockSpec((MOD_ROWS, bn), lambda j: (0, j)),
        out_shape=jax.ShapeDtypeStruct((MOD_ROWS, n), F32),
        compiler_params=_params(("arbitrary",)),
        name="ada_mod",
    )(c_all, w_ada, b_ada.reshape(1, n))


def _prenorm_kernel(x_ref, ctx_ref, mod_ref, g_ref, o_ref, *, n_ctx_blk, d):
    def emit(src_ref):
        for r0 in range(0, o_ref.shape[0], PRENORM_CHUNK):
            rows = slice(r0, r0 + PRENORM_CHUNK)
            y = _rms(src_ref[rows, :], g_ref[...])
            o_ref[rows, :] = (y * (1.0 + mod_ref[:, d:2 * d]) + mod_ref[:, 0:d]).astype(o_ref.dtype)

    @pl.when(pl.program_id(0) < n_ctx_blk)
    def _():
        emit(ctx_ref)

    @pl.when(pl.program_id(0) >= n_ctx_blk)
    def _():
        emit(x_ref)


def _row_block(b, blk, bsz, n_ctx_blk, n_x_blk):
    return jnp.where(blk < n_ctx_blk, b * n_ctx_blk + blk, bsz * n_ctx_blk + b * n_x_blk + blk - n_ctx_blk)


def _prenorm(x2, ctx2, mod3, g, rb, *, bsz, s_len):
    t, d = x2.shape
    m_ctx = ctx2.shape[0]
    n_ctx_blk, blk_per_batch = m_ctx // rb, s_len // rb

    def lat(i):
        return jnp.maximum(i - n_ctx_blk, 0)

    return pl.pallas_call(
        functools.partial(_prenorm_kernel, n_ctx_blk=n_ctx_blk, d=d),
        grid=((m_ctx + t) // rb,),
        in_specs=[
            pl.BlockSpec((rb, d), lambda i: (lat(i), 0)),
            pl.BlockSpec((rb, d), lambda i: (jnp.minimum(i, n_ctx_blk - 1), 0)),
            pl.BlockSpec((None, 1, 2 * d), lambda i: (jnp.where(i < n_ctx_blk, bsz, lat(i) // blk_per_batch), 0, 0)),
            pl.BlockSpec((1, d), lambda i: (0, 0)),
        ],
        out_specs=pl.BlockSpec((rb, d), lambda i: (i, 0)),
        out_shape=jax.ShapeDtypeStruct((m_ctx + t, d), BF16),
        compiler_params=_params(("arbitrary",)),
        name="prenorm_mod",
    )(x2, ctx2, mod3, g)


CAST_ROWS = 64


def _cast_specs(src, nsteps, lin):
    rows, cols = src.shape
    cr = CAST_ROWS
    while rows % cr or rows // cr > nsteps:
        cr += CAST_ROWS
    nblk = rows // cr
    spec = pl.BlockSpec((cr, cols), lambda *g: (jnp.minimum(lin(*g), nblk - 1), 0))
    return spec, spec, jax.ShapeDtypeStruct((rows, cols), BF16)


_NT = (((1,), (1,)), ((), ()))


def _qkproj_kernel(a_ref, w_ref, cos_ref, sin_ref, z_ref, *, dk):
    r = lax.dot_general(a_ref[...], w_ref[...], _NT, preferred_element_type=F32)
    cos, sin = cos_ref[...], sin_ref[...]
    for h in range(r.shape[1] // dk):
        cs = slice(h * dk, (h + 1) * dk)
        z_ref[:, cs] = (r[:, cs] * cos + pltpu.roll(r[:, cs], dk // 2, 1) * sin).astype(z_ref.dtype)


def _qkproj(a, w_t, cos_t, sin_t, bm, bn, *, dk, m_ctx, s_len):
    m, k = a.shape
    n = w_t.shape[0]
    n_ctx_tiles, tiles_per_batch = m_ctx // bm, s_len // bm
    assert m_ctx % bm == 0 and s_len % bm == 0

    def table_tile(i, j):
        return jnp.where(i < n_ctx_tiles, i, n_ctx_tiles + (i - n_ctx_tiles) % tiles_per_batch), 0

    return pl.pallas_call(
        functools.partial(_qkproj_kernel, dk=dk),
        grid=(m // bm, n // bn),
        in_specs=[pl.BlockSpec((bm, k), lambda i, j: (i, 0)),
                  pl.BlockSpec((bn, k), lambda i, j: (j, 0)),
                  pl.BlockSpec((bm, dk), table_tile),
                  pl.BlockSpec((bm, dk), table_tile)],
        out_specs=pl.BlockSpec((bm, bn), lambda i, j: (i, j)),
        out_shape=jax.ShapeDtypeStruct((m, n), BF16),
        compiler_params=_params(("arbitrary", "arbitrary")),
        name="in_proj_qk",
    )(a, w_t, cos_t, sin_t)


def _inproj_kernel(a_ref, w_ref, wlr_ref, src_ref, z_ref, lr_ref, dst_ref):
    a = a_ref[...]
    z_ref[...] = lax.dot_general(a, w_ref[...], _NT, preferred_element_type=F32).astype(z_ref.dtype)
    dst_ref[...] = src_ref[...].astype(dst_ref.dtype)

    @pl.when(pl.program_id(1) == 0)
    def _():
        lr_ref[...] = lax.dot_general(a, wlr_ref[...], _NT, preferred_element_type=F32)


def _inproj(a, w_t, row0, n, wlr_t, cast_src, bm, bn):
    m, k = a.shape
    ni, nj, j0 = m // bm, n // bn, row0 // bn
    assert row0 % bn == 0
    c_in, c_out, c_shape = _cast_specs(cast_src, ni * nj, lambda i, j: i * nj + j)
    return pl.pallas_call(
        _inproj_kernel,
        grid=(ni, nj),
        in_specs=[pl.BlockSpec((bm, k), lambda i, j: (i, 0)),
                  pl.BlockSpec((bn, k), lambda i, j: (j0 + j, 0)),
                  pl.BlockSpec((V7X_LANES, k), lambda i, j: (0, 0)),
                  c_in],
        out_specs=[pl.BlockSpec((bm, bn), lambda i, j: (i, j)),
                   pl.BlockSpec((bm, V7X_LANES), lambda i, j: (i, 0)),
                   c_out],
        out_shape=[jax.ShapeDtypeStruct((m, n), BF16),
                   jax.ShapeDtypeStruct((m, V7X_LANES), F32),
                   c_shape],
        compiler_params=_params(("arbitrary", "arbitrary")),
        name="in_proj_vr",
    )(a, w_t, wlr_t, cast_src)


def _matmul_kernel(a_ref, w_ref, *rest, nk, act, cast, w_t):
    if cast:
        src_ref, o_ref, dst_ref = rest[:3]
        dst_ref[...] = src_ref[...].astype(dst_ref.dtype)
    else:
        o_ref = rest[0]

    def product():
        if w_t:
            return lax.dot_general(a_ref[...], w_ref[...], _NT, preferred_element_type=F32)
        return jnp.dot(a_ref[...], w_ref[...], preferred_element_type=F32)

    def finish(r):
        if act == "relu2":
            r = jnp.square(jnp.maximum(r, 0.0))
        elif act == "gelu":
            r = _gelu(r)
        o_ref[...] = r.astype(o_ref.dtype)

    if nk == 1:
        finish(product())
        return
    acc_ref = rest[-1]
    kk = pl.program_id(2)

    @pl.when(kk == 0)
    def _():
        acc_ref[...] = product()

    @pl.when(jnp.logical_and(kk > 0, kk < nk - 1))
    def _():
        acc_ref[...] += product()

    @pl.when(kk == nk - 1)
    def _():
        finish(acc_ref[...] + product())


def _matmul(a, w, out_dtype, bm, bn, bk, act=None, cast_src=None, w_t=False, a_row0=0, w_row0=0, n=None,
            name="matmul"):
    k = a.shape[1]
    m = a.shape[0] - a_row0
    if n is None:
        n = w.shape[0] if w_t else w.shape[1]
    ni, nj, nk, i0 = m // bm, n // bn, k // bk, a_row0 // bm
    assert a_row0 % bm == 0 and (w_t or w_row0 == 0)
    if not w_t:
        w_spec = pl.BlockSpec((bk, bn), lambda i, j, kk: (kk, j))
    elif w_row0 % bn == 0:
        w_spec = pl.BlockSpec((bn, bk), lambda i, j, kk: (w_row0 // bn + j, kk))
    else:
        assert w_row0 % 16 == 0 and bn % 16 == 0, "bf16 rows come in sublane pairs of 8"
        w_spec = pl.BlockSpec((pl.Element(bn), pl.Element(bk)),
                              lambda i, j, kk: (pl.multiple_of(w_row0 + j * bn, 16), pl.multiple_of(kk * bk, 128)))
    in_specs = [pl.BlockSpec((bm, bk), lambda i, j, kk: (i0 + i, kk)), w_spec]
    out_specs = [pl.BlockSpec((bm, bn), lambda i, j, kk: (i, j))]
    out_shape = [jax.ShapeDtypeStruct((m, n), out_dtype)]
    args = [a, w]
    if cast_src is not None:
        c_in, c_out, c_shape = _cast_specs(cast_src, ni * nj * nk, lambda i, j, kk: (i * nj + j) * nk + kk)
        in_specs.append(c_in)
        out_specs.append(c_out)
        out_shape.append(c_shape)
        args.append(cast_src)
    res = pl.pallas_call(
        functools.partial(_matmul_kernel, nk=nk, act=act, cast=cast_src is not None, w_t=w_t),
        grid=(ni, nj, nk),
        in_specs=in_specs,
        out_specs=out_specs,
        out_shape=out_shape,
        scratch_shapes=[pltpu.VMEM((bm, bn), F32)] if nk > 1 else [],
        compiler_params=_params(("arbitrary", "arbitrary", "arbitrary")),
        name=name,
    )(*args)
    return res if cast_src is not None else res[0]


_Dir = collections.namedtuple("_Dir", "q k v lr wd bd cmat o st reverse")


def _gla_block(dirs, emit, *, hb, dk, dv, tb):
    nch = tb // GLA_CHUNK
    nt = _NT
    tn = (((0,), (0,)), ((), ()))

    def ks(h):
        return slice(h * dk, (h + 1) * dk)

    def vs(h):
        return slice(h * dv, (h + 1) * dv)

    def rows(c):
        return slice(c * GLA_CHUNK, (c + 1) * GLA_CHUNK)

    gates = []
    for d in dirs:
        a = jnp.dot(d.lr[...].astype(BF16), d.wd[...].astype(BF16), preferred_element_type=F32) + d.bd[...]
        la = (jnp.minimum(a, 0.0) - jnp.log(1.0 + jnp.exp(-jnp.abs(a)))) * (LOG2_E / GLA_TAU)
        hi = la.astype(BF16)
        lo = (la - hi.astype(F32)).astype(BF16)
        tri = d.cmat[...]
        gates.append((jnp.dot(tri, hi, preferred_element_type=F32)
                      + jnp.dot(tri, lo, preferred_element_type=F32), tri))

    work = []
    for d, (cum, tri) in zip(dirs, gates):
        kr = d.k[...].astype(F32)
        ends = [c * GLA_CHUNK if d.reverse else (c + 1) * GLA_CHUNK - 1 for c in range(nch)]
        tot = [cum[e:e + 1] for e in ends]
        kd = [(kr[rows(c)] * jnp.exp2(tot[c] - cum[rows(c)])).astype(BF16) for c in range(nch)]
        qe = ke = None
        if emit:
            qe = (d.q[...].astype(F32) * (jnp.exp2(cum) * dk ** -0.5)).astype(BF16)
            ke = (kr * jnp.exp2(-cum)).astype(BF16)
        work.append((d, d.v[...], qe, ke, kd, tot, tri))

    heads = range(hb)
    intra = {}
    if emit:
        att = {(i, h): lax.dot_general(qe[:, ks(h)], ke[:, ks(h)], nt, preferred_element_type=F32)
               for i, (_, _, qe, ke, _, _, _) in enumerate(work) for h in heads}
        for i, (_, v, _, _, _, _, tri) in enumerate(work):
            for h in heads:
                att_m = jnp.where(tri > 0, att[i, h].astype(BF16), jnp.zeros((), BF16))
                intra[i, h] = jnp.dot(att_m, v[:, vs(h)], preferred_element_type=F32)

    upd = {(i, h, c): lax.dot_general(v[rows(c), vs(h)], kd[c][:, ks(h)], tn, preferred_element_type=F32)
           for i, (_, v, _, _, kd, _, _) in enumerate(work) for h in heads for c in range(nch)}

    seen = {}
    for i, (d, _, _, _, _, tot, _) in enumerate(work):
        order = range(nch - 1, -1, -1) if d.reverse else range(nch)
        for h in heads:
            st = d.st[h]
            for c in order:
                if emit:
                    seen[i, h, c] = st.astype(BF16)
                st = st * jnp.exp2(tot[c][:, ks(h)]) + upd[i, h, c]
            d.st[h] = st

    if emit:
        for i, (d, _, qe, _, _, _, _) in enumerate(work):
            for h in heads:
                for c in range(nch):
                    inter = lax.dot_general(qe[rows(c), ks(h)], seen[i, h, c], nt, preferred_element_type=F32)
                    d.o[rows(c), vs(h)] = (intra[i, h][rows(c)] + inter).astype(d.o.dtype)


def _gla_kernel(qf, kf, vf, lrf, qb, kb, vb, lrb,
                wdf, bdf, wdb, bdb, cmf, cmb, of_ref, ob_ref, stf, stb, *, n_ctx_blk, **kw):
    s = pl.program_id(2)
    dirs = (_Dir(qf, kf, vf, lrf, wdf, bdf, cmf, of_ref, stf, False),
            _Dir(qb, kb, vb, lrb, wdb, bdb, cmb, ob_ref, stb, True))

    @pl.when(s == 0)
    def _():
        stf[...] = jnp.zeros_like(stf)
        stb[...] = jnp.zeros_like(stb)

    @pl.when(s < n_ctx_blk)
    def _():
        _gla_block(dirs, False, **kw)

    @pl.when(s >= n_ctx_blk)
    def _():
        _gla_block(dirs, True, **kw)


def _gla(zqk, zvr, lr, wdf, bdf, wdb, bdb, cmf, cmb, *, bsz, s_len, c_len, tb, hb, dk, dv):
    n_ctx_blk, n_x_blk = c_len // tb, s_len // tb
    nb = n_ctx_blk + n_x_blk
    key_w = GLA_HEADS * dk
    qw, vw = hb * dk, hb * dv
    k_off = key_w // qw

    def fblk(s):
        return s

    def bblk(s):
        return jnp.where(s < n_ctx_blk, n_ctx_blk - 1 - s, nb - 1 - s + n_ctx_blk)

    def specs(blk):
        def row(b, s):
            return _row_block(b, blk(s), bsz, n_ctx_blk, n_x_blk)

        return [
            pl.BlockSpec((tb, qw), lambda b, g, s: (row(b, s), g)),
            pl.BlockSpec((tb, qw), lambda b, g, s: (row(b, s), k_off + g)),
            pl.BlockSpec((tb, vw), lambda b, g, s: (row(b, s), g)),
            pl.BlockSpec((tb, V7X_LANES), lambda b, g, s: (row(b, s), 0)),
        ]

    def wspec():
        return [pl.BlockSpec((V7X_LANES, qw), lambda b, g, s: (0, g)),
                pl.BlockSpec((1, qw), lambda b, g, s: (0, g))]

    def oblk_f(b, g, s):
        return (b * n_x_blk + jnp.maximum(s - n_ctx_blk, 0), g)

    def oblk_b(b, g, s):
        return (b * n_x_blk + jnp.minimum(nb - 1 - s + n_ctx_blk, nb - 1) - n_ctx_blk, g)

    cspec = pl.BlockSpec((tb, tb), lambda b, g, s: (0, 0))
    kern = functools.partial(_gla_kernel, n_ctx_blk=n_ctx_blk, hb=hb, dk=dk, dv=dv, tb=tb)
    o_shape = jax.ShapeDtypeStruct((bsz * s_len, GLA_HEADS * dv), BF16)
    return pl.pallas_call(
        kern,
        grid=(bsz, GLA_HEADS // hb, nb),
        in_specs=specs(fblk) + specs(bblk) + wspec() + wspec() + [cspec, cspec],
        out_specs=[pl.BlockSpec((tb, vw), oblk_f), pl.BlockSpec((tb, vw), oblk_b)],
        out_shape=[o_shape, o_shape],
        scratch_shapes=[pltpu.VMEM((hb, dv, dk), F32), pltpu.VMEM((hb, dv, dk), F32)],
        compiler_params=_params(("arbitrary", "arbitrary", "arbitrary")),
        name="gla_scan",
    )(zqk, zqk, zvr, lr, zqk, zqk, zvr, lr, wdf, bdf, wdb, bdb, cmf, cmb)


def _mix_kernel(of_ref, ob_ref, r_ref, u_ref, vv_ref, gn_ref, lg_ref, lb_ref, ws_ref, bs_ref, y_ref,
                *, dv, val_w, groups, tb):
    o = of_ref[...].astype(F32) + ob_ref[...].astype(F32)
    for h in range(val_w // dv):
        cs = slice(h * dv, (h + 1) * dv)
        r = r_ref[:, cs].astype(F32)
        y_ref[:, cs] = (_rms(o[:, cs], gn_ref[:, cs]) * _silu(r)).astype(y_ref.dtype)

    u = u_ref[...].astype(F32)
    vv = vv_ref[...].astype(F32)
    mu = jnp.mean(vv, axis=-1, keepdims=True)
    cen = vv - mu
    var = jnp.mean(cen * cen, axis=-1, keepdims=True)
    vn = (cen * lax.rsqrt(var + EPS) * lg_ref[...] + lb_ref[...]).astype(BF16)
    sgw = vn.shape[1]
    gw = sgw // groups
    for c in range(tb // SG_CHUNK):
        rows = slice(c * SG_CHUNK, (c + 1) * SG_CHUNK)
        for g in range(groups):
            cs = slice(g * gw, (g + 1) * gw)
            sg = jnp.dot(ws_ref[g], vn[rows, cs], preferred_element_type=F32) + bs_ref[:, cs]
            y_ref[rows, val_w + g * gw:val_w + (g + 1) * gw] = (u[rows, cs] * sg).astype(y_ref.dtype)


def _mix(o_f, o_b, z, zg, gn, lg, lb, ws, bsx, *, bsz, s_len, c_len, tb, dv, val_w, sgw, r_off):
    t = bsz * s_len
    groups = ws.shape[0]

    def zrow(i):
        return bsz * c_len // tb + i

    kern = functools.partial(_mix_kernel, dv=dv, val_w=val_w, groups=groups, tb=tb)
    return pl.pallas_call(
        kern,
        grid=(t // tb,),
        in_specs=[pl.BlockSpec((tb, val_w), lambda i: (i, 0)),
                  pl.BlockSpec((tb, val_w), lambda i: (i, 0)),
                  pl.BlockSpec((tb, val_w), lambda i: (zrow(i), r_off // val_w)),
                  pl.BlockSpec((tb, sgw), lambda i: (i, 0)),
                  pl.BlockSpec((tb, sgw), lambda i: (i, 1)),
                  pl.BlockSpec((1, val_w), lambda i: (0, 0)),
                  pl.BlockSpec((1, sgw), lambda i: (0, 0)),
                  pl.BlockSpec((1, sgw), lambda i: (0, 0)),
                  pl.BlockSpec((groups, SG_CHUNK, SG_CHUNK), lambda i: (0, 0, 0)),
                  pl.BlockSpec((SG_CHUNK, sgw), lambda i: (0, 0))],
        out_specs=pl.BlockSpec((tb, val_w + sgw), lambda i: (i, 0)),
        out_shape=jax.ShapeDtypeStruct((t, val_w + sgw), BF16),
        compiler_params=_params(("arbitrary",)),
        name="mix_readout",
    )(o_f, o_b, z, zg, zg, gn, lg, lb, ws, bsx)


def _res1_kernel(x_ref, m_ref, g1_ref, sh_ref, sc_ref, pg_ref, ng_ref, x1_ref, h2_ref):
    x1 = x_ref[...] + g1_ref[...] * _rms(m_ref[...].astype(F32), pg_ref[...])
    x1_ref[...] = x1
    h2_ref[...] = (_rms(x1, ng_ref[...]) * (1.0 + sc_ref[...]) + sh_ref[...]).astype(h2_ref.dtype)


def _mod_spec(n_x, d, col):
    return pl.BlockSpec((None, 1, d), lambda i: (i // n_x, 0, col))


def _res1(x2, mix, mod3, post_g, pre_g, *, s_len, tb):
    t, d = x2.shape
    n_x = s_len // tb
    row = pl.BlockSpec((tb, d), lambda i: (i, 0))
    vec = pl.BlockSpec((1, d), lambda i: (0, 0))
    return pl.pallas_call(
        _res1_kernel,
        grid=(t // tb,),
        in_specs=[row, row, _mod_spec(n_x, d, 2), _mod_spec(n_x, d, 3), _mod_spec(n_x, d, 4), vec, vec],
        out_specs=[row, row],
        out_shape=[jax.ShapeDtypeStruct((t, d), F32), jax.ShapeDtypeStruct((t, d), BF16)],
        compiler_params=_params(("arbitrary",)),
        name="residual1",
    )(x2, mix, mod3, mod3, mod3, post_g, pre_g)


def _res2_kernel(x_ref, m_ref, g2_ref, pg_ref, o_ref):
    o_ref[...] = x_ref[...] + g2_ref[...] * _rms(m_ref[...].astype(F32), pg_ref[...])


def _res2(x1, m2, mod3, post_g, *, s_len, tb):
    t, d = x1.shape
    n_x = s_len // tb
    row = pl.BlockSpec((tb, d), lambda i: (i, 0))
    return pl.pallas_call(
        _res2_kernel,
        grid=(t // tb,),
        in_specs=[row, row, _mod_spec(n_x, d, 5), pl.BlockSpec((1, d), lambda i: (0, 0))],
        out_specs=row,
        out_shape=jax.ShapeDtypeStruct((t, d), F32),
        compiler_params=_params(("arbitrary",)),
        name="residual2",
    )(x1, m2, mod3, post_g)


def _pair_order(t, axis, dk):
    shp = t.shape
    t = t.reshape(shp[:axis] + (shp[axis] // dk, 2, 2, dk // 4) + shp[axis + 1:])
    return jnp.swapaxes(t, axis + 1, axis + 2).reshape(shp)


def _rope_tables(s_len, m_ctx, dk):
    m = dk // 4
    inv_freq = ROPE_BASE ** (-jnp.arange(m, dtype=F32) / m)
    pos = jnp.arange(s_len)
    ang_r = (pos // GRID_W).astype(F32)[:, None] * inv_freq[None, :]
    ang_c = (pos % GRID_W).astype(F32)[:, None] * inv_freq[None, :]
    cos = jnp.concatenate([jnp.cos(ang_r), jnp.cos(ang_c)] * 2, axis=-1)
    sin = jnp.concatenate([-jnp.sin(ang_r), -jnp.sin(ang_c), jnp.sin(ang_r), jnp.sin(ang_c)], axis=-1)
    return (jnp.concatenate([jnp.ones((m_ctx, dk), F32), cos], axis=0),
            jnp.concatenate([jnp.zeros((m_ctx, dk), F32), sin], axis=0))


def _chunk_matrices(tb):
    i = jnp.arange(tb)
    same = (i[:, None] // GLA_CHUNK) == (i[None, :] // GLA_CHUNK)
    return ((same & (i[None, :] <= i[:, None])).astype(BF16),
            (same & (i[None, :] >= i[:, None])).astype(BF16))


def _tile(n, pref):
    return pref if n % pref == 0 else n


def kernel(x, c, ctx, c_ctx, w_ada, b_ada, pre1_g, post1_g, pre2_g, post2_g, w_in, w_dec_f, b_dec_f,
           w_dec_b, b_dec_b, gla_norm_g, sg_ln_g, sg_ln_b, w_s, b_s, w_o, w_1, w_2):
    bsz, s_len, d = x.shape
    c_len = ctx.shape[1]
    assert w_ada.shape[0] == 1, "single-layer block only"
    assert bsz < MOD_ROWS
    lowrank, key_w = w_dec_f.shape[1], w_dec_f.shape[2]
    dk = key_w // GLA_HEADS
    dv = gla_norm_g.shape[2]
    val_w = GLA_HEADS * dv
    sgw = sg_ln_g.shape[1]
    lf0 = 2 * key_w + 2 * val_w
    sg0 = lf0 + 2 * lowrank
    assert dk == V7X_LANES and 2 * lowrank <= V7X_LANES and w_in.shape[2] == sg0 + 2 * sgw
    t = bsz * s_len

    w_in_t = jnp.swapaxes(w_in[0], 0, 1).astype(BF16)
    w_qk_t = _pair_order(w_in_t[:2 * key_w], 0, dk)
    w_lr_t = jnp.pad(w_in_t[lf0:sg0], ((0, V7X_LANES - 2 * lowrank), (0, 0)))
    wdf = jnp.pad(_pair_order(w_dec_f[0], 1, dk), ((0, V7X_LANES - lowrank), (0, 0)))
    wdb = jnp.pad(_pair_order(w_dec_b[0], 1, dk), ((lowrank, V7X_LANES - 2 * lowrank), (0, 0)))
    bdf, bdb = _pair_order(b_dec_f, 1, dk), _pair_order(b_dec_b, 1, dk)

    c_all = jnp.concatenate([c, c_ctx[None], jnp.zeros((MOD_ROWS - bsz - 1, d), F32)], axis=0)
    mod = _ada(c_all, w_ada[0], b_ada[0])
    mod3 = mod.reshape(MOD_ROWS, 1, N_MOD * d)

    rb = _tile(math.gcd(s_len, c_len), ROW_BLOCK)
    mm = MATMUL_TILE
    x2 = x.reshape(t, d)
    m_ctx = bsz * c_len
    rl = _tile(math.gcd(m_ctx, s_len), LATENT_ROW_BLOCK)
    hx = _prenorm(x2, ctx.reshape(m_ctx, d), mod3, pre1_g, rl, bsz=bsz, s_len=s_len)

    bm_all = _tile(m_ctx + t, mm)
    cos_t, sin_t = _rope_tables(s_len, m_ctx, dk)
    zqk = _qkproj(hx, w_qk_t, cos_t, sin_t, _tile(math.gcd(m_ctx, s_len), mm), _tile(2 * key_w, mm),
                  dk=dk, m_ctx=m_ctx, s_len=s_len)
    zvr, lr, w_o_b = _inproj(hx, w_in_t, 2 * key_w, 2 * val_w, w_lr_t, w_o[0], bm_all,
                             _tile(math.gcd(2 * key_w, 2 * val_w), mm))
    zg = _matmul(hx, w_in_t, BF16, _tile(math.gcd(m_ctx, t), mm), _tile(2 * sgw, mm), d, act="gelu",
                 w_t=True, a_row0=m_ctx, w_row0=sg0, n=2 * sgw, name="in_proj_sg")

    cmf, cmb = _chunk_matrices(rb)
    o_f, o_b = _gla(zqk, zvr, lr, wdf, bdf, wdb, bdb, cmf, cmb,
                    bsz=bsz, s_len=s_len, c_len=c_len, tb=rb, hb=GLA_HEADS, dk=dk, dv=dv)

    bsx = jnp.repeat(b_s[0].T, sgw // w_s.shape[1], axis=1)
    y = _mix(o_f, o_b, zvr, zg, gla_norm_g.reshape(1, val_w), sg_ln_g, sg_ln_b, w_s[0].astype(BF16), bsx,
             bsz=bsz, s_len=s_len, c_len=c_len, tb=rl, dv=dv, val_w=val_w, sgw=sgw, r_off=val_w)

    bm = _tile(t, mm)
    mix, w_1_b = _matmul(y, w_o_b, BF16, bm, _tile(d, mm), d, cast_src=w_1[0], name="out_proj")
    x1, h2 = _res1(x2, mix, mod3, post1_g, pre2_g, s_len=s_len, tb=rb)
    d_ff = w_1_b.shape[1]
    hmid, w_2_b = _matmul(h2, w_1_b, BF16, bm, _tile(d_ff, mm), d, act="relu2", cast_src=w_2[0],
                          name="mlp_up")
    m2 = _matmul(hmid, w_2_b, BF16, bm, _tile(d, mm), _tile(d_ff, MLP_DOWN_K_TILE), name="mlp_down")
    out = _res2(x1, m2, mod3, post2_g, s_len=s_len, tb=rl)
    return out.reshape(bsz, s_len, d)
```

```python
import collections
import functools
import math

import jax
import jax.numpy as jnp
from jax import lax
from jax.experimental import pallas as pl
from jax.experimental.pallas import tpu as pltpu

GRID_W = 64
GLA_HEADS = 8
GLA_TAU = 16.0
ROPE_BASE = 10000.0
N_MOD = 6
EPS = 1e-6
SG_CHUNK = 128
GLA_CHUNK = 64
LOG2_E = math.log2(math.e)

V7X_LANES = 128
MOD_ROWS = 8
VMEM_LIMIT_BYTES = 56 * 1024 * 1024
MATMUL_TILE = 1024
MLP_DOWN_K_TILE = 4096
ROW_BLOCK = 256
LATENT_ROW_BLOCK = 512
PRENORM_CHUNK = 128

F32 = jnp.float32
BF16 = jnp.bfloat16


def _params(sem, vmem=VMEM_LIMIT_BYTES):
    return pltpu.CompilerParams(dimension_semantics=sem, vmem_limit_bytes=vmem)


def _rms(t, g):
    return t * lax.rsqrt(jnp.mean(t * t, axis=-1, keepdims=True) + EPS) * g


def _silu(t):
    return t * (1.0 / (1.0 + jnp.exp(-t)))


def _gelu(t):
    return 0.5 * t * (1.0 + lax.erf(t * (2.0 ** -0.5)))


def _ada_kernel(c_ref, w_ref, b_ref, o_ref):
    cond = _silu(c_ref[...])
    o_ref[...] = jnp.dot(cond.astype(BF16), w_ref[...].astype(BF16),
                         preferred_element_type=F32) + b_ref[...]


def _ada(c_all, w_ada, b_ada):
    d, n = w_ada.shape
    bn = _tile(n, MATMUL_TILE)
    return pl.pallas_call(
        _ada_kernel,
        grid=(n // bn,),
        in_specs=[pl.BlockSpec((MOD_ROWS, d), lambda j: (0, 0)),
                  pl.BlockSpec((d, bn), lambda j: (0, j)),
                  pl.BlockSpec((1, bn), lambda j: (0, j))],
        out_specs=pl.BlockSpec((MOD_ROWS, bn), lambda j: (0, j)),
        out_shape=jax.ShapeDtypeStruct((MOD_ROWS, n), F32),
        compiler_params=_params(("arbitrary",)),
        name="ada_mod",
    )(c_all, w_ada, b_ada.reshape(1, n))


def _prenorm_kernel(x_ref, ctx_ref, mod_ref, g_ref, o_ref, *, n_ctx_blk, d):
    def emit(src_ref):
        for r0 in range(0, o_ref.shape[0], PRENORM_CHUNK):
            rows = slice(r0, r0 + PRENORM_CHUNK)
            y = _rms(src_ref[rows, :], g_ref[...])
            o_ref[rows, :] = (y * (1.0 + mod_ref[:, d:2 * d]) + mod_ref[:, 0:d]).astype(o_ref.dtype)

    @pl.when(pl.program_id(0) < n_ctx_blk)
    def _():
        emit(ctx_ref)

    @pl.when(pl.program_id(0) >= n_ctx_blk)
    def _():
        emit(x_ref)


def _row_block(b, blk, bsz, n_ctx_blk, n_x_blk):
    return jnp.where(blk < n_ctx_blk, b * n_ctx_blk + blk, bsz * n_ctx_blk + b * n_x_blk + blk - n_ctx_blk)


def _prenorm(x2, ctx2, mod3, g, rb, *, bsz, s_len):
    t, d = x2.shape
    m_ctx = ctx2.shape[0]
    n_ctx_blk, blk_per_batch = m_ctx // rb, s_len // rb

    def lat(i):
        return jnp.maximum(i - n_ctx_blk, 0)

    return pl.pallas_call(
        functools.partial(_prenorm_kernel, n_ctx_blk=n_ctx_blk, d=d),
        grid=((m_ctx + t) // rb,),
        in_specs=[
            pl.BlockSpec((rb, d), lambda i: (lat(i), 0)),
            pl.BlockSpec((rb, d), lambda i: (jnp.minimum(i, n_ctx_blk - 1), 0)),
            pl.BlockSpec((None, 1, 2 * d), lambda i: (jnp.where(i < n_ctx_blk, bsz, lat(i) // blk_per_batch), 0, 0)),
            pl.BlockSpec((1, d), lambda i: (0, 0)),
        ],
        out_specs=pl.BlockSpec((rb, d), lambda i: (i, 0)),
        out_shape=jax.ShapeDtypeStruct((m_ctx + t, d), BF16),
        compiler_params=_params(("arbitrary",)),
        name="prenorm_mod",
    )(x2, ctx2, mod3, g)


CAST_ROWS = 64


def _cast_specs(src, nsteps, lin):
    rows, cols = src.shape
    cr = CAST_ROWS
    while rows % cr or rows // cr > nsteps:
        cr += CAST_ROWS
    nblk = rows // cr
    spec = pl.BlockSpec((cr, cols), lambda *g: (jnp.minimum(lin(*g), nblk - 1), 0))
    return spec, spec, jax.ShapeDtypeStruct((rows, cols), BF16)


_NT = (((1,), (1,)), ((), ()))


def _qkproj_kernel(a_ref, w_ref, cos_ref, sin_ref, z_ref, *, dk):
    r = lax.dot_general(a_ref[...], w_ref[...], _NT, preferred_element_type=F32)
    cos, sin = cos_ref[...], sin_ref[...]
    for h in range(r.shape[1] // dk):
        cs = slice(h * dk, (h + 1) * dk)
        z_ref[:, cs] = (r[:, cs] * cos + pltpu.roll(r[:, cs], dk // 2, 1) * sin).astype(z_ref.dtype)


def _qkproj(a, w_t, cos_t, sin_t, bm, bn, *, dk, m_ctx, s_len):
    m, k = a.shape
    n = w_t.shape[0]
    n_ctx_tiles, tiles_per_batch = m_ctx // bm, s_len // bm
    assert m_ctx % bm == 0 and s_len % bm == 0

    def table_tile(i, j):
        return jnp.where(i < n_ctx_tiles, i, n_ctx_tiles + (i - n_ctx_tiles) % tiles_per_batch), 0

    return pl.pallas_call(
        functools.partial(_qkproj_kernel, dk=dk),
        grid=(m // bm, n // bn),
        in_specs=[pl.BlockSpec((bm, k), lambda i, j: (i, 0)),
                  pl.BlockSpec((bn, k), lambda i, j: (j, 0)),
                  pl.BlockSpec((bm, dk), table_tile),
                  pl.BlockSpec((bm, dk), table_tile)],
        out_specs=pl.BlockSpec((bm, bn), lambda i, j: (i, j)),
        out_shape=jax.ShapeDtypeStruct((m, n), BF16),
        compiler_params=_params(("arbitrary", "arbitrary")),
        name="in_proj_qk",
    )(a, w_t, cos_t, sin_t)


def _inproj_kernel(a_ref, w_ref, wlr_ref, src_ref, z_ref, lr_ref, dst_ref):
    a = a_ref[...]
    z_ref[...] = lax.dot_general(a, w_ref[...], _NT, preferred_element_type=F32).astype(z_ref.dtype)
    dst_ref[...] = src_ref[...].astype(dst_ref.dtype)

    @pl.when(pl.program_id(1) == 0)
    def _():
        lr_ref[...] = lax.dot_general(a, wlr_ref[...], _NT, preferred_element_type=F32)


def _inproj(a, w_t, row0, n, wlr_t, cast_src, bm, bn):
    m, k = a.shape
    ni, nj, j0 = m // bm, n // bn, row0 // bn
    assert row0 % bn == 0
    c_in, c_out, c_shape = _cast_specs(cast_src, ni * nj, lambda i, j: i * nj + j)
    return pl.pallas_call(
        _inproj_kernel,
        grid=(ni, nj),
        in_specs=[pl.BlockSpec((bm, k), lambda i, j: (i, 0)),
                  pl.BlockSpec((bn, k), lambda i, j: (j0 + j, 0)),
                  pl.BlockSpec((V7X_LANES, k), lambda i, j: (0, 0)),
                  c_in],
        out_specs=[pl.BlockSpec((bm, bn), lambda i, j: (i, j)),
                   pl.BlockSpec((bm, V7X_LANES), lambda i, j: (i, 0)),
                   c_out],
        out_shape=[jax.ShapeDtypeStruct((m, n), BF16),
                   jax.ShapeDtypeStruct((m, V7X_LANES), F32),
                   c_shape],
        compiler_params=_params(("arbitrary", "arbitrary")),
        name="in_proj_vr",
    )(a, w_t, wlr_t, cast_src)


def _matmul_kernel(a_ref, w_ref, *rest, nk, act, cast, w_t):
    if cast:
        src_ref, o_ref, dst_ref = rest[:3]
        dst_ref[...] = src_ref[...].astype(dst_ref.dtype)
    else:
        o_ref = rest[0]

    def product():
        if w_t:
            return lax.dot_general(a_ref[...], w_ref[...], _NT, preferred_element_type=F32)
        return jnp.dot(a_ref[...], w_ref[...], preferred_element_type=F32)

    def finish(r):
        if act == "relu2":
            r = jnp.square(jnp.maximum(r, 0.0))
        elif act == "gelu":
            r = _gelu(r)
        o_ref[...] = r.astype(o_ref.dtype)

    if nk == 1:
        finish(product())
        return
    acc_ref = rest[-1]
    kk = pl.program_id(2)

    @pl.when(kk == 0)
    def _():
        acc_ref[...] = product()

    @pl.when(jnp.logical_and(kk > 0, kk < nk - 1))
    def _():
        acc_ref[...] += product()

    @pl.when(kk == nk - 1)
    def _():
        finish(acc_ref[...] + product())


def _matmul(a, w, out_dtype, bm, bn, bk, act=None, cast_src=None, w_t=False, a_row0=0, w_row0=0, n=None,
            name="matmul"):
    k = a.shape[1]
    m = a.shape[0] - a_row0
    if n is None:
        n = w.shape[0] if w_t else w.shape[1]
    ni, nj, nk, i0 = m // bm, n // bn, k // bk, a_row0 // bm
    assert a_row0 % bm == 0 and (w_t or w_row0 == 0)
    if not w_t:
        w_spec = pl.BlockSpec((bk, bn), lambda i, j, kk: (kk, j))
    elif w_row0 % bn == 0:
        w_spec = pl.BlockSpec((bn, bk), lambda i, j, kk: (w_row0 // bn + j, kk))
    else:
        assert w_row0 % 16 == 0 and bn % 16 == 0, "bf16 rows come in sublane pairs of 8"
        w_spec = pl.BlockSpec((pl.Element(bn), pl.Element(bk)),
                              lambda i, j, kk: (pl.multiple_of(w_row0 + j * bn, 16), pl.multiple_of(kk * bk, 128)))
    in_specs = [pl.BlockSpec((bm, bk), lambda i, j, kk: (i0 + i, kk)), w_spec]
    out_specs = [pl.BlockSpec((bm, bn), lambda i, j, kk: (i, j))]
    out_shape = [jax.ShapeDtypeStruct((m, n), out_dtype)]
    args = [a, w]
    if cast_src is not None:
        c_in, c_out, c_shape = _cast_specs(cast_src, ni * nj * nk, lambda i, j, kk: (i * nj + j) * nk + kk)
        in_specs.append(c_in)
        out_specs.append(c_out)
        out_shape.append(c_shape)
        args.append(cast_src)
    res = pl.pallas_call(
        functools.partial(_matmul_kernel, nk=nk, act=act, cast=cast_src is not None, w_t=w_t),
        grid=(ni, nj, nk),
        in_specs=in_specs,
        out_specs=out_specs,
        out_shape=out_shape,
        scratch_shapes=[pltpu.VMEM((bm, bn), F32)] if nk > 1 else [],
        compiler_params=_params(("arbitrary", "arbitrary", "arbitrary")),
        name=name,
    )(*args)
    return res if cast_src is not None else res[0]


_Dir = collections.namedtuple("_Dir", "q k v lr wd bd cmat o st reverse")


def _gla_block(dirs, emit, *, hb, dk, dv, tb):
    nch = tb // GLA_CHUNK
    nt = _NT
    tn = (((0,), (0,)), ((), ()))

    def ks(h):
        return slice(h * dk, (h + 1) * dk)

    def vs(h):
        return slice(h * dv, (h + 1) * dv)

    def rows(c):
        return slice(c * GLA_CHUNK, (c + 1) * GLA_CHUNK)

    gates = []
    for d in dirs:
        a = jnp.dot(d.lr[...].astype(BF16), d.wd[...].astype(BF16), preferred_element_type=F32) + d.bd[...]
        la = (jnp.minimum(a, 0.0) - jnp.log(1.0 + jnp.exp(-jnp.abs(a)))) * (LOG2_E / GLA_TAU)
        hi = la.astype(BF16)
        lo = (la - hi.astype(F32)).astype(BF16)
        tri = d.cmat[...]
        gates.append((jnp.dot(tri, hi, preferred_element_type=F32)
                      + jnp.dot(tri, lo, preferred_element_type=F32), tri))

    work = []
    for d, (cum, tri) in zip(dirs, gates):
        kr = d.k[...].astype(F32)
        ends = [c * GLA_CHUNK if d.reverse else (c + 1) * GLA_CHUNK - 1 for c in range(nch)]
        tot = [cum[e:e + 1] for e in ends]
        kd = [(kr[rows(c)] * jnp.exp2(tot[c] - cum[rows(c)])).astype(BF16) for c in range(nch)]
        qe = qa = ke = None
        if emit:
            q = d.q[...].astype(F32) * dk ** -0.5
            qe = (q * jnp.exp2(cum)).astype(BF16)
            half = jnp.concatenate([jnp.broadcast_to(0.5 * tot[c], (GLA_CHUNK, cum.shape[1])) for c in range(nch)],
                                   axis=0)
            qa = (q * jnp.exp2(cum - half)).astype(BF16)
            ke = (kr * jnp.exp2(half - cum)).astype(BF16)
        work.append((d, d.v[...], qe, qa, ke, kd, tot, tri))

    heads = range(hb)
    intra = {}
    if emit:
        att = {(i, h): lax.dot_general(qa[:, ks(h)], ke[:, ks(h)], nt, preferred_element_type=F32)
               for i, (_, _, _, qa, ke, _, _, _) in enumerate(work) for h in heads}
        for i, (_, v, _, _, _, _, _, tri) in enumerate(work):
            for h in heads:
                att_m = jnp.where(tri > 0, att[i, h].astype(BF16), jnp.zeros((), BF16))
                intra[i, h] = jnp.dot(att_m, v[:, vs(h)], preferred_element_type=F32)

    upd = {(i, h, c): lax.dot_general(v[rows(c), vs(h)], kd[c][:, ks(h)], tn, preferred_element_type=F32)
           for i, (_, v, _, _, _, kd, _, _) in enumerate(work) for h in heads for c in range(nch)}

    seen = {}
    for i, (d, _, _, _, _, _, tot, _) in enumerate(work):
        order = range(nch - 1, -1, -1) if d.reverse else range(nch)
        for h in heads:
            st = d.st[h]
            for c in order:
                if emit:
                    seen[i, h, c] = st.astype(BF16)
                st = st * jnp.exp2(tot[c][:, ks(h)]) + upd[i, h, c]
            d.st[h] = st

    if emit:
        for i, (d, _, qe, _, _, _, _, _) in enumerate(work):
            for h in heads:
                for c in range(nch):
                    inter = lax.dot_general(qe[rows(c), ks(h)], seen[i, h, c], nt, preferred_element_type=F32)
                    d.o[rows(c), vs(h)] = (intra[i, h][rows(c)] + inter).astype(d.o.dtype)


def _gla_kernel(qf, kf, vf, lrf, qb, kb, vb, lrb,
                wdf, bdf, wdb, bdb, cmf, cmb, of_ref, ob_ref, stf, stb, *, n_ctx_blk, **kw):
    s = pl.program_id(2)
    dirs = (_Dir(qf, kf, vf, lrf, wdf, bdf, cmf, of_ref, stf, False),
            _Dir(qb, kb, vb, lrb, wdb, bdb, cmb, ob_ref, stb, True))

    @pl.when(s == 0)
    def _():
        stf[...] = jnp.zeros_like(stf)
        stb[...] = jnp.zeros_like(stb)

    @pl.when(s < n_ctx_blk)
    def _():
        _gla_block(dirs, False, **kw)

    @pl.when(s >= n_ctx_blk)
    def _():
        _gla_block(dirs, True, **kw)


def _gla(zqk, zvr, lr, wdf, bdf, wdb, bdb, cmf, cmb, *, bsz, s_len, c_len, tb, hb, dk, dv):
    n_ctx_blk, n_x_blk = c_len // tb, s_len // tb
    nb = n_ctx_blk + n_x_blk
    key_w = GLA_HEADS * dk
    qw, vw = hb * dk, hb * dv
    k_off = key_w // qw

    def fblk(s):
        return s

    def bblk(s):
        return jnp.where(s < n_ctx_blk, n_ctx_blk - 1 - s, nb - 1 - s + n_ctx_blk)

    def specs(blk):
        def row(b, s):
            return _row_block(b, blk(s), bsz, n_ctx_blk, n_x_blk)

        return [
            pl.BlockSpec((tb, qw), lambda b, g, s: (row(b, s), g)),
            pl.BlockSpec((tb, qw), lambda b, g, s: (row(b, s), k_off + g)),
            pl.BlockSpec((tb, vw), lambda b, g, s: (row(b, s), g)),
            pl.BlockSpec((tb, V7X_LANES), lambda b, g, s: (row(b, s), 0)),
        ]

    def wspec():
        return [pl.BlockSpec((V7X_LANES, qw), lambda b, g, s: (0, g)),
                pl.BlockSpec((1, qw), lambda b, g, s: (0, g))]

    def oblk_f(b, g, s):
        return (b * n_x_blk + jnp.maximum(s - n_ctx_blk, 0), g)

    def oblk_b(b, g, s):
        return (b * n_x_blk + jnp.minimum(nb - 1 - s + n_ctx_blk, nb - 1) - n_ctx_blk, g)

    cspec = pl.BlockSpec((tb, tb), lambda b, g, s: (0, 0))
    kern = functools.partial(_gla_kernel, n_ctx_blk=n_ctx_blk, hb=hb, dk=dk, dv=dv, tb=tb)
    o_shape = jax.ShapeDtypeStruct((bsz * s_len, GLA_HEADS * dv), BF16)
    return pl.pallas_call(
        kern,
        grid=(bsz, GLA_HEADS // hb, nb),
        in_specs=specs(fblk) + specs(bblk) + wspec() + wspec() + [cspec, cspec],
        out_specs=[pl.BlockSpec((tb, vw), oblk_f), pl.BlockSpec((tb, vw), oblk_b)],
        out_shape=[o_shape, o_shape],
        scratch_shapes=[pltpu.VMEM((hb, dv, dk), F32), pltpu.VMEM((hb, dv, dk), F32)],
        compiler_params=_params(("arbitrary", "arbitrary", "arbitrary")),
        name="gla_scan",
    )(zqk, zqk, zvr, lr, zqk, zqk, zvr, lr, wdf, bdf, wdb, bdb, cmf, cmb)


def _mix_kernel(of_ref, ob_ref, r_ref, u_ref, vv_ref, gn_ref, lg_ref, lb_ref, ws_ref, bs_ref, y_ref,
                *, dv, val_w, groups, tb):
    o = of_ref[...].astype(F32) + ob_ref[...].astype(F32)
    for h in range(val_w // dv):
        cs = slice(h * dv, (h + 1) * dv)
        r = r_ref[:, cs].astype(F32)
        y_ref[:, cs] = (_rms(o[:, cs], gn_ref[:, cs]) * _silu(r)).astype(y_ref.dtype)

    u = u_ref[...].astype(F32)
    vv = vv_ref[...].astype(F32)
    mu = jnp.mean(vv, axis=-1, keepdims=True)
    cen = vv - mu
    var = jnp.mean(cen * cen, axis=-1, keepdims=True)
    vn = (cen * lax.rsqrt(var + EPS) * lg_ref[...] + lb_ref[...]).astype(BF16)
    sgw = vn.shape[1]
    gw = sgw // groups
    for c in range(tb // SG_CHUNK):
        rows = slice(c * SG_CHUNK, (c + 1) * SG_CHUNK)
        for g in range(groups):
            cs = slice(g * gw, (g + 1) * gw)
            sg = jnp.dot(ws_ref[g], vn[rows, cs], preferred_element_type=F32) + bs_ref[:, cs]
            y_ref[rows, val_w + g * gw:val_w + (g + 1) * gw] = (u[rows, cs] * sg).astype(y_ref.dtype)


def _mix(o_f, o_b, z, zg, gn, lg, lb, ws, bsx, *, bsz, s_len, c_len, tb, dv, val_w, sgw, r_off):
    t = bsz * s_len
    groups = ws.shape[0]

    def zrow(i):
        return bsz * c_len // tb + i

    kern = functools.partial(_mix_kernel, dv=dv, val_w=val_w, groups=groups, tb=tb)
    return pl.pallas_call(
        kern,
        grid=(t // tb,),
        in_specs=[pl.BlockSpec((tb, val_w), lambda i: (i, 0)),
                  pl.BlockSpec((tb, val_w), lambda i: (i, 0)),
                  pl.BlockSpec((tb, val_w), lambda i: (zrow(i), r_off // val_w)),
                  pl.BlockSpec((tb, sgw), lambda i: (i, 0)),
                  pl.BlockSpec((tb, sgw), lambda i: (i, 1)),
                  pl.BlockSpec((1, val_w), lambda i: (0, 0)),
                  pl.BlockSpec((1, sgw), lambda i: (0, 0)),
                  pl.BlockSpec((1, sgw), lambda i: (0, 0)),
                  pl.BlockSpec((groups, SG_CHUNK, SG_CHUNK), lambda i: (0, 0, 0)),
                  pl.BlockSpec((SG_CHUNK, sgw), lambda i: (0, 0))],
        out_specs=pl.BlockSpec((tb, val_w + sgw), lambda i: (i, 0)),
        out_shape=jax.ShapeDtypeStruct((t, val_w + sgw), BF16),
        compiler_params=_params(("arbitrary",)),
        name="mix_readout",
    )(o_f, o_b, z, zg, zg, gn, lg, lb, ws, bsx)


def _res1_kernel(x_ref, m_ref, g1_ref, sh_ref, sc_ref, pg_ref, ng_ref, x1_ref, h2_ref):
    x1 = x_ref[...] + g1_ref[...] * _rms(m_ref[...].astype(F32), pg_ref[...])
    x1_ref[...] = x1
    h2_ref[...] = (_rms(x1, ng_ref[...]) * (1.0 + sc_ref[...]) + sh_ref[...]).astype(h2_ref.dtype)


def _mod_spec(n_x, d, col):
    return pl.BlockSpec((None, 1, d), lambda i: (i // n_x, 0, col))


def _res1(x2, mix, mod3, post_g, pre_g, *, s_len, tb):
    t, d = x2.shape
    n_x = s_len // tb
    row = pl.BlockSpec((tb, d), lambda i: (i, 0))
    vec = pl.BlockSpec((1, d), lambda i: (0, 0))
    return pl.pallas_call(
        _res1_kernel,
        grid=(t // tb,),
        in_specs=[row, row, _mod_spec(n_x, d, 2), _mod_spec(n_x, d, 3), _mod_spec(n_x, d, 4), vec, vec],
        out_specs=[row, row],
        out_shape=[jax.ShapeDtypeStruct((t, d), F32), jax.ShapeDtypeStruct((t, d), BF16)],
        compiler_params=_params(("arbitrary",)),
        name="residual1",
    )(x2, mix, mod3, mod3, mod3, post_g, pre_g)


def _res2_kernel(x_ref, m_ref, g2_ref, pg_ref, o_ref):
    o_ref[...] = x_ref[...] + g2_ref[...] * _rms(m_ref[...].astype(F32), pg_ref[...])


def _res2(x1, m2, mod3, post_g, *, s_len, tb):
    t, d = x1.shape
    n_x = s_len // tb
    row = pl.BlockSpec((tb, d), lambda i: (i, 0))
    return pl.pallas_call(
        _res2_kernel,
        grid=(t // tb,),
        in_specs=[row, row, _mod_spec(n_x, d, 5), pl.BlockSpec((1, d), lambda i: (0, 0))],
        out_specs=row,
        out_shape=jax.ShapeDtypeStruct((t, d), F32),
        compiler_params=_params(("arbitrary",)),
        name="residual2",
    )(x1, m2, mod3, post_g)


def _pair_order(t, axis, dk):
    shp = t.shape
    t = t.reshape(shp[:axis] + (shp[axis] // dk, 2, 2, dk // 4) + shp[axis + 1:])
    return jnp.swapaxes(t, axis + 1, axis + 2).reshape(shp)


def _rope_tables(s_len, m_ctx, dk):
    m = dk // 4
    inv_freq = ROPE_BASE ** (-jnp.arange(m, dtype=F32) / m)
    pos = jnp.arange(s_len)
    ang_r = (pos // GRID_W).astype(F32)[:, None] * inv_freq[None, :]
    ang_c = (pos % GRID_W).astype(F32)[:, None] * inv_freq[None, :]
    cos = jnp.concatenate([jnp.cos(ang_r), jnp.cos(ang_c)] * 2, axis=-1)
    sin = jnp.concatenate([-jnp.sin(ang_r), -jnp.sin(ang_c), jnp.sin(ang_r), jnp.sin(ang_c)], axis=-1)
    return (jnp.concatenate([jnp.ones((m_ctx, dk), F32), cos], axis=0),
            jnp.concatenate([jnp.zeros((m_ctx, dk), F32), sin], axis=0))


def _chunk_matrices(tb):
    i = jnp.arange(tb)
    same = (i[:, None] // GLA_CHUNK) == (i[None, :] // GLA_CHUNK)
    return ((same & (i[None, :] <= i[:, None])).astype(BF16),
            (same & (i[None, :] >= i[:, None])).astype(BF16))


def _tile(n, pref):
    return pref if n % pref == 0 else n


def kernel(x, c, ctx, c_ctx, w_ada, b_ada, pre1_g, post1_g, pre2_g, post2_g, w_in, w_dec_f, b_dec_f,
           w_dec_b, b_dec_b, gla_norm_g, sg_ln_g, sg_ln_b, w_s, b_s, w_o, w_1, w_2):
    bsz, s_len, d = x.shape
    c_len = ctx.shape[1]
    assert w_ada.shape[0] == 1, "single-layer block only"
    assert bsz < MOD_ROWS
    lowrank, key_w = w_dec_f.shape[1], w_dec_f.shape[2]
    dk = key_w // GLA_HEADS
    dv = gla_norm_g.shape[2]
    val_w = GLA_HEADS * dv
    sgw = sg_ln_g.shape[1]
    lf0 = 2 * key_w + 2 * val_w
    sg0 = lf0 + 2 * lowrank
    assert dk == V7X_LANES and 2 * lowrank <= V7X_LANES and w_in.shape[2] == sg0 + 2 * sgw
    t = bsz * s_len

    w_in_t = jnp.swapaxes(w_in[0], 0, 1).astype(BF16)
    w_qk_t = _pair_order(w_in_t[:2 * key_w], 0, dk)
    w_lr_t = jnp.pad(w_in_t[lf0:sg0], ((0, V7X_LANES - 2 * lowrank), (0, 0)))
    wdf = jnp.pad(_pair_order(w_dec_f[0], 1, dk), ((0, V7X_LANES - lowrank), (0, 0)))
    wdb = jnp.pad(_pair_order(w_dec_b[0], 1, dk), ((lowrank, V7X_LANES - 2 * lowrank), (0, 0)))
    bdf, bdb = _pair_order(b_dec_f, 1, dk), _pair_order(b_dec_b, 1, dk)

    c_all = jnp.concatenate([c, c_ctx[None], jnp.zeros((MOD_ROWS - bsz - 1, d), F32)], axis=0)
    mod = _ada(c_all, w_ada[0], b_ada[0])
    mod3 = mod.reshape(MOD_ROWS, 1, N_MOD * d)

    rb = _tile(math.gcd(s_len, c_len), ROW_BLOCK)
    mm = MATMUL_TILE
    x2 = x.reshape(t, d)
    m_ctx = bsz * c_len
    rl = _tile(math.gcd(m_ctx, s_len), LATENT_ROW_BLOCK)
    hx = _prenorm(x2, ctx.reshape(m_ctx, d), mod3, pre1_g, rl, bsz=bsz, s_len=s_len)

    bm_all = _tile(m_ctx + t, mm)
    cos_t, sin_t = _rope_tables(s_len, m_ctx, dk)
    zqk = _qkproj(hx, w_qk_t, cos_t, sin_t, _tile(math.gcd(m_ctx, s_len), mm), _tile(2 * key_w, mm),
                  dk=dk, m_ctx=m_ctx, s_len=s_len)
    zvr, lr, w_o_b = _inproj(hx, w_in_t, 2 * key_w, 2 * val_w, w_lr_t, w_o[0], bm_all,
                             _tile(math.gcd(2 * key_w, 2 * val_w), mm))
    zg = _matmul(hx, w_in_t, BF16, _tile(math.gcd(m_ctx, t), mm), _tile(2 * sgw, mm), d, act="gelu",
                 w_t=True, a_row0=m_ctx, w_row0=sg0, n=2 * sgw, name="in_proj_sg")

    cmf, cmb = _chunk_matrices(rb)
    o_f, o_b = _gla(zqk, zvr, lr, wdf, bdf, wdb, bdb, cmf, cmb,
                    bsz=bsz, s_len=s_len, c_len=c_len, tb=rb, hb=GLA_HEADS, dk=dk, dv=dv)

    bsx = jnp.repeat(b_s[0].T, sgw // w_s.shape[1], axis=1)
    y = _mix(o_f, o_b, zvr, zg, gla_norm_g.reshape(1, val_w), sg_ln_g, sg_ln_b, w_s[0].astype(BF16), bsx,
             bsz=bsz, s_len=s_len, c_len=c_len, tb=rl, dv=dv, val_w=val_w, sgw=sgw, r_off=val_w)

    bm = _tile(t, mm)
    mix, w_1_b = _matmul(y, w_o_b, BF16, bm, _tile(d, mm), d, cast_src=w_1[0], name="out_proj")
    x1, h2 = _res1(x2, mix, mod3, post1_g, pre2_g, s_len=s_len, tb=rb)
    d_ff = w_1_b.shape[1]
    hmid, w_2_b = _matmul(h2, w_1_b, BF16, bm, _tile(d_ff, mm), d, act="relu2", cast_src=w_2[0],
                          name="mlp_up")
    m2 = _matmul(hmid, w_2_b, BF16, bm, _tile(d, mm), _tile(d_ff, MLP_DOWN_K_TILE), name="mlp_down")
    out = _res2(x1, m2, mod3, post2_g, s_len=s_len, tb=rl)
    return out.reshape(bsz, s_len, d)
```
